```python
import math
import jax, jax.numpy as jnp
from jax import lax
import numpy as np

D_MODEL = 1024
BATCH = 16
SEQ = 4096
DEPTH = 1
DEC_BATCH = 2
DEC_SEQ = 8192
PAST_LEN = 128

A_HEADS = 8
A_QK_DIM = 64
A_V_DIM = 2 * A_QK_DIM
A_QK_WIDTH = 2 * A_HEADS * A_QK_DIM
A_WIDTH = A_HEADS * A_V_DIM
B_HEADS = 8
B_KV_HEADS = 2
B_GROUP = B_HEADS // B_KV_HEADS
B_HEAD_DIM = 128
B_WIDTH = B_HEADS * B_HEAD_DIM
B_KV_WIDTH = B_KV_HEADS * B_HEAD_DIM
ROPE_AXIS_DIM = B_HEAD_DIM // 2
ROPE_THETA = 10000.0
GRID_W = 64
REL_BUCKETS = 32
REL_MAX_DIST = 128
Q_BLOCK = 128
EPS = 1e-6
IN_SIZES = (A_QK_WIDTH, A_QK_WIDTH, A_WIDTH, A_WIDTH, B_WIDTH, B_KV_WIDTH, B_KV_WIDTH, B_WIDTH, D_MODEL, D_MODEL)
IN_WIDTH = 2 * A_QK_WIDTH + 2 * A_WIDTH + 2 * B_WIDTH + 2 * B_KV_WIDTH + 2 * D_MODEL

kernel_name = "hybrid_gated_diffattn_gqa_axialrope_encoder"


def _rms_norm(x, g):
    x32 = x.astype(jnp.float32)
    y = x32 * lax.rsqrt(jnp.mean(x32 * x32, axis=-1, keepdims=True) + EPS)
    return (y * g.astype(jnp.float32)).astype(x.dtype)


def _to_blocks(q):
    shp = q.shape
    q = q.reshape(shp[:-2] + (shp[-2] // Q_BLOCK, Q_BLOCK, shp[-1]))
    return jnp.moveaxis(q, -3, 0)


def _from_blocks(o):
    o = jnp.moveaxis(o, 0, -3)
    shp = o.shape
    return o.reshape(shp[:-3] + (shp[-3] * shp[-2], shp[-1]))


def _rel_bucket(rel):
    half = REL_BUCKETS // 2
    max_exact = half // 2
    ret = (rel > 0).astype(jnp.int32) * half
    n = jnp.abs(rel)
    nf = jnp.maximum(n, 1).astype(jnp.float32)
    large = max_exact + (jnp.log(nf / max_exact) / math.log(REL_MAX_DIST / max_exact)
                         * (half - max_exact)).astype(jnp.int32)
    large = jnp.minimum(large, half - 1)
    return ret + jnp.where(n < max_exact, n, large)


def _diff_attention(q1, q2, k1, k2, v, rel_bias, lam):
    S = q1.shape[2]
    nb = S // Q_BLOCK
    scale = A_QK_DIM ** -0.5
    kpos = jnp.arange(S, dtype=jnp.int32)

    def blk(args):
        i, qa, qb = args
        qpos = i * Q_BLOCK + jnp.arange(Q_BLOCK, dtype=jnp.int32)
        bucket = _rel_bucket(kpos[None, :] - qpos[:, None])
        bias = jnp.transpose(rel_bias[bucket], (2, 0, 1)).astype(jnp.float32)
        s1 = jnp.einsum('bhqd,bhkd->bhqk', qa, k1).astype(jnp.float32) * scale + bias
        s2 = jnp.einsum('bhqd,bhkd->bhqk', qb, k2).astype(jnp.float32) * scale + bias
        p = jax.nn.softmax(s1, axis=-1) - lam * jax.nn.softmax(s2, axis=-1)
        return jnp.einsum('bhqk,bhkd->bhqd', p.astype(v.dtype), v)

    out = lax.map(blk, (jnp.arange(nb, dtype=jnp.int32), _to_blocks(q1), _to_blocks(q2)))
    return _from_blocks(out)


def _gqa_attention(q, k, v):
    scale = B_HEAD_DIM ** -0.5

    def blk(qb):
        s = jnp.einsum('bngqd,bnkd->bngqk', qb, k).astype(jnp.float32) * scale
        p = jax.nn.softmax(s, axis=-1)
        return jnp.einsum('bngqk,bnkd->bngqd', p.astype(v.dtype), v)

    return _from_blocks(lax.map(blk, _to_blocks(q)))


def _axial_angles(S):
    rows = S // GRID_W
    row_idx = jnp.repeat(jnp.arange(rows, dtype=jnp.float32), GRID_W)
    col_idx = jnp.tile(jnp.arange(GRID_W, dtype=jnp.float32), rows)
    inv_freq = ROPE_THETA ** (-jnp.arange(0, ROPE_AXIS_DIM, 2, dtype=jnp.float32) / ROPE_AXIS_DIM)
    ang_r = row_idx[:, None] * inv_freq[None, :]
    ang_c = col_idx[:, None] * inv_freq[None, :]
    return jnp.cos(ang_r), jnp.sin(ang_r), jnp.cos(ang_c), jnp.sin(ang_c)


def _rot(u, c, s):
    h = u.shape[-1] // 2
    u1, u2 = u[..., :h], u[..., h:]
    return jnp.concatenate([u1 * c - u2 * s, u1 * s + u2 * c], axis=-1)


def _apply_axial_rope(x, angles):
    cr, sr, cc, sc = [a[:, None, :] for a in angles]
    x32 = x.astype(jnp.float32)
    out = jnp.concatenate([_rot(x32[..., :ROPE_AXIS_DIM], cr, sr),
                           _rot(x32[..., ROPE_AXIS_DIM:], cc, sc)], axis=-1)
    return out.astype(x.dtype)


def _layer(x, l, g_norm, w_in, lambda_q1, lambda_k1, lambda_q2, lambda_k2, subln_w,
           q_norm_b, k_norm_b, w_proj_a, w_proj_b, w_out, rel_bias):
    B, S, _ = x.shape
    xn = _rms_norm(x, g_norm)
    proj = xn @ w_in
    offs = [int(o) for o in np.cumsum(IN_SIZES)[:-1]]
    qa, ka, va, za, qb, kb, vb, zb, ga, gb = jnp.split(proj, offs, axis=-1)

    lambda_init = 0.8 - 0.6 * math.exp(-0.3 * l)
    qa = jnp.transpose(qa.reshape(B, S, A_HEADS, 2, A_QK_DIM), (3, 0, 2, 1, 4))
    ka = jnp.transpose(ka.reshape(B, S, A_HEADS, 2, A_QK_DIM), (3, 0, 2, 1, 4))
    va = jnp.transpose(va.reshape(B, S, A_HEADS, A_V_DIM), (0, 2, 1, 3))
    lam = (jnp.exp(jnp.sum(lambda_q1.astype(jnp.float32) * lambda_k1.astype(jnp.float32)))
           - jnp.exp(jnp.sum(lambda_q2.astype(jnp.float32) * lambda_k2.astype(jnp.float32)))
           + lambda_init)
    oa = _diff_attention(qa[0], qa[1], ka[0], ka[1], va, rel_bias, lam)
    oa = _rms_norm(oa, subln_w) * (1.0 - lambda_init)
    oa = jnp.transpose(oa, (0, 2, 1, 3)).reshape(B, S, A_WIDTH)
    ya = (oa * jax.nn.silu(za)) @ w_proj_a

    angles = _axial_angles(S)
    qb = _apply_axial_rope(_rms_norm(qb.reshape(B, S, B_HEADS, B_HEAD_DIM), q_norm_b), angles)
    kb = _apply_axial_rope(_rms_norm(kb.reshape(B, S, B_KV_HEADS, B_HEAD_DIM), k_norm_b), angles)
    qb = jnp.transpose(qb.reshape(B, S, B_KV_HEADS, B_GROUP, B_HEAD_DIM), (0, 2, 3, 1, 4))
    kb = jnp.transpose(kb, (0, 2, 1, 3))
    vb = jnp.transpose(vb.reshape(B, S, B_KV_HEADS, B_HEAD_DIM), (0, 2, 1, 3))
    ob = _gqa_attention(qb, kb, vb)
    ob = jnp.transpose(ob, (0, 3, 1, 2, 4)).reshape(B, S, B_WIDTH)
    yb = (ob * jax.nn.silu(zb)) @ w_proj_b

    merged = jax.nn.sigmoid(ga) * ya + jax.nn.sigmoid(gb) * yb
    return x + merged @ w_out


def _trunk(x, g_norm, w_in, lambda_q1, lambda_k1, lambda_q2, lambda_k2, subln_w,
           q_norm_b, k_norm_b, w_proj_a, w_proj_b, w_out, rel_bias, g_final):
    h = x
    for l in range(DEPTH):
        h = _layer(h, l, g_norm[l], w_in[l], lambda_q1[l], lambda_k1[l], lambda_q2[l], lambda_k2[l],
                   subln_w[l], q_norm_b[l], k_norm_b[l], w_proj_a[l], w_proj_b[l], w_out[l], rel_bias)
    return _rms_norm(h, g_final)


def setup_inputs(seed: int = 0) -> dict:
    key = jax.random.key(seed)
    ks = jax.random.split(key, 18)
    f32 = jnp.float32
    nrm = lambda k, shp, s: jax.random.normal(k, shp, f32) * s
    return {
        "x_prompt": nrm(ks[0], (BATCH, SEQ, D_MODEL), 1.0),
        "x_sample": nrm(ks[1], (DEC_BATCH, DEC_SEQ, D_MODEL), 1.0),
        "g_norm": 1.0 + nrm(ks[2], (DEPTH, D_MODEL), 0.02),
        "w_in": nrm(ks[3], (DEPTH, D_MODEL, IN_WIDTH), D_MODEL ** -0.5),
        "lambda_q1": nrm(ks[4], (DEPTH, A_QK_DIM), 0.1),
        "lambda_k1": nrm(ks[5], (DEPTH, A_QK_DIM), 0.1),
        "lambda_q2": nrm(ks[6], (DEPTH, A_QK_DIM), 0.1),
        "lambda_k2": nrm(ks[7], (DEPTH, A_QK_DIM), 0.1),
        "subln_w": 1.0 + nrm(ks[8], (DEPTH, A_V_DIM), 0.02),
        "q_norm_b": 1.0 + nrm(ks[9], (DEPTH, B_HEAD_DIM), 0.02),
        "k_norm_b": 1.0 + nrm(ks[10], (DEPTH, B_HEAD_DIM), 0.02),
        "w_proj_a": nrm(ks[11], (DEPTH, A_WIDTH, D_MODEL), A_WIDTH ** -0.5),
        "w_proj_b": nrm(ks[12], (DEPTH, B_WIDTH, D_MODEL), B_WIDTH ** -0.5),
        "w_out": nrm(ks[13], (DEPTH, D_MODEL, D_MODEL), D_MODEL ** -0.5),
        "rel_bias": nrm(ks[14], (REL_BUCKETS, A_HEADS), 0.3),
        "g_final": 1.0 + nrm(ks[15], (D_MODEL,), 0.02),
    }


def reference(x_prompt, x_sample, g_norm, w_in, lambda_q1, lambda_k1, lambda_q2, lambda_k2, subln_w,
              q_norm_b, k_norm_b, w_proj_a, w_proj_b, w_out, rel_bias, g_final):
    y_prompt = _trunk(x_prompt, g_norm, w_in, lambda_q1, lambda_k1, lambda_q2, lambda_k2, subln_w,
                      q_norm_b, k_norm_b, w_proj_a, w_proj_b, w_out, rel_bias, g_final)
    y_sample = _trunk(x_sample, g_norm, w_in, lambda_q1, lambda_k1, lambda_q2, lambda_k2, subln_w,
                      q_norm_b, k_norm_b, w_proj_a, w_proj_b, w_out, rel_bias, g_final)
    return (y_prompt, y_sample)
```

```python
import functools
import math

import jax
import jax.numpy as jnp
from jax import lax
from jax.experimental import pallas as pl
from jax.experimental.pallas import tpu as pltpu

D_MODEL = 1024
A_HEADS = 8
A_QK_DIM = 64
A_V_DIM = 128
B_HEADS = 8
B_KV_HEADS = 2
B_GROUP = B_HEADS // B_KV_HEADS
B_HEAD_DIM = 128
HEAD_W = 128
ROPE_AXIS_DIM = 64
ROPE_THETA = 10000.0
GRID_W = 64
REL_BUCKETS = 32
REL_MAX_DIST = 128
EPS = 1e-6
DEPTH = 1
IN_SIZES = (1024, 1024, 1024, 1024, 1024, 256, 256, 1024, 1024, 1024)

BIAS_SATURATION_DIST = 128

VMEM_LIMIT_BYTES = 56 * 1024 * 1024

F32 = jnp.float32
BF16 = jnp.bfloat16


def _compiler_params(semantics):
    return pltpu.CompilerParams(dimension_semantics=semantics, vmem_limit_bytes=VMEM_LIMIT_BYTES)


def _rel_bucket(rel):
    half = REL_BUCKETS // 2
    max_exact = half // 2
    ret = (rel > 0).astype(jnp.int32) * half
    n = jnp.abs(rel)
    nf = jnp.maximum(n, 1).astype(F32)
    large = max_exact + (jnp.log(nf / max_exact) / math.log(REL_MAX_DIST / max_exact)
                         * (half - max_exact)).astype(jnp.int32)
    large = jnp.minimum(large, half - 1)
    return ret + jnp.where(n < max_exact, n, large)


def _num_bias_tiles(tq, tk):
    u = min(tq, tk)
    return tk // u + tq // u + 3


def _bias_bucket_tiles(tq, tk):
    u = min(tq, tk)
    assert tq % u == 0 and tk % u == 0 and u >= BIAS_SATURATION_DIST
    nt = _num_bias_tiles(tq, tk)
    o = (jnp.arange(nt, dtype=jnp.int32) - (tk // u + 1)) * u
    kk = jnp.arange(tk, dtype=jnp.int32)
    qq = jnp.arange(tq, dtype=jnp.int32)
    rel = o[:, None, None] + kk[None, :, None] - qq[None, None, :]
    return _rel_bucket(rel)


def _bias_tiles_kernel(bucket_ref, rel_bias_ref, out_ref):
    h = pl.program_id(0)
    bucket = bucket_ref[0]
    acc = jnp.zeros(bucket.shape, F32)
    for b in range(REL_BUCKETS):
        acc = jnp.where(bucket == b, rel_bias_ref[b, h], acc)
    out_ref[0, 0] = acc


def _bias_tiles(rel_bias, tq, tk):
    buckets = _bias_bucket_tiles(tq, tk)
    nt = buckets.shape[0]
    return pl.pallas_call(
        _bias_tiles_kernel,
        out_shape=jax.ShapeDtypeStruct((A_HEADS, nt, tk, tq), F32),
        grid=(A_HEADS, nt),
        in_specs=[
            pl.BlockSpec((1, tk, tq), lambda h, t: (t, 0, 0)),
            pl.BlockSpec(memory_space=pltpu.SMEM),
        ],
        out_specs=pl.BlockSpec((1, 1, tk, tq), lambda h, t: (h, t, 0, 0)),
        compiler_params=_compiler_params(("arbitrary", "arbitrary")),
        name="bias_tiles",
    )(buckets, rel_bias.astype(F32))


def _axial_tables(S):
    rows = S // GRID_W
    row_idx = jnp.repeat(jnp.arange(rows, dtype=F32), GRID_W)
    col_idx = jnp.tile(jnp.arange(GRID_W, dtype=F32), rows)
    inv_freq = ROPE_THETA ** (-jnp.arange(0, ROPE_AXIS_DIM, 2, dtype=F32) / ROPE_AXIS_DIM)
    ang_r = row_idx[:, None] * inv_freq[None, :]
    ang_c = col_idx[:, None] * inv_freq[None, :]
    cr, sr, cc, sc = jnp.cos(ang_r), jnp.sin(ang_r), jnp.cos(ang_c), jnp.sin(ang_c)
    cos_t = jnp.concatenate([cr, cr, cc, cc], axis=-1)
    sin_t = jnp.concatenate([-sr, sr, -sc, sc], axis=-1)
    return cos_t, sin_t


def _head_rms(t, gain):
    ms = jnp.mean(t * t, axis=-1, keepdims=True)
    return t * lax.rsqrt(ms + EPS) * gain


def _rope(t, cos_t, sin_t):
    lane = lax.broadcasted_iota(jnp.int32, t.shape, 1)
    quarter = ROPE_AXIS_DIM // 2
    partner = jnp.where((lane & quarter) == 0,
                        pltpu.roll(t, HEAD_W - quarter, 1), pltpu.roll(t, quarter, 1))
    return t * cos_t + partner * sin_t


def _in_proj_kernel(x_ref, g_ref, cos_ref, sin_ref, qn_ref, kn_ref,
                    w_qa, w_ka, w_va, w_za, w_qb, w_kb, w_vb, w_zb, w_ga, w_gb,
                    qaT_ref, ka_ref, vaT_ref, za_ref, qbT_ref, kb_ref, vbT_ref, zb_ref, ga_ref, gb_ref):
    x = x_ref[...]
    ms = jnp.mean(x * x, axis=-1, keepdims=True)
    xn = (x * lax.rsqrt(ms + EPS) * g_ref[...]).astype(BF16)

    def proj(w_ref):
        return jnp.dot(xn, w_ref[...], preferred_element_type=F32)

    def head(p, h):
        return p[:, h * HEAD_W:(h + 1) * HEAD_W]

    p = proj(w_qa) * (A_QK_DIM ** -0.5)
    for h in range(A_HEADS):
        qaT_ref[0, h] = head(p, h).T.astype(BF16)
    ka_ref[...] = proj(w_ka).astype(BF16)
    p = proj(w_va)
    for h in range(A_HEADS):
        vaT_ref[0, h] = head(p, h).T.astype(BF16)
    p = proj(w_za)
    za_ref[...] = (p * jax.nn.sigmoid(p)).astype(BF16)

    cos_t = cos_ref[...]
    sin_t = sin_ref[...]
    p = proj(w_qb)
    for h in range(B_HEADS):
        t = _rope(_head_rms(head(p, h), qn_ref[...]), cos_t, sin_t) * (B_HEAD_DIM ** -0.5)
        qbT_ref[0, h] = t.T.astype(BF16)
    p = proj(w_kb)
    for h in range(B_KV_HEADS):
        t = _rope(_head_rms(head(p, h), kn_ref[...]), cos_t, sin_t)
        kb_ref[:, h * HEAD_W:(h + 1) * HEAD_W] = t.astype(BF16)
    p = proj(w_vb)
    for h in range(B_KV_HEADS):
        vbT_ref[0, h] = head(p, h).T.astype(BF16)
    p = proj(w_zb)
    zb_ref[...] = (p * jax.nn.sigmoid(p)).astype(BF16)
    ga_ref[...] = jax.nn.sigmoid(proj(w_ga)).astype(BF16)
    gb_ref[...] = jax.nn.sigmoid(proj(w_gb)).astype(BF16)


def _in_proj(x, g_norm, w_sections, q_norm_b, k_norm_b, tm):
    B, S, D = x.shape
    T = B * S
    assert S % tm == 0
    tiles_per_seq = S // tm
    cos_t, sin_t = _axial_tables(S)
    x2 = x.reshape(T, D)

    def tok_spec(width):
        return pl.BlockSpec((tm, width), lambda t: (t, 0))

    def headT_spec(nh):
        return pl.BlockSpec((1, nh, HEAD_W, tm), lambda t: (t // tiles_per_seq, 0, 0, t % tiles_per_seq))

    def const_spec(shape):
        return pl.BlockSpec(shape, lambda t: (0,) * len(shape), pipeline_mode=pl.Buffered(1))

    rope_spec = pl.BlockSpec((tm, HEAD_W), lambda t: (t % tiles_per_seq, 0))
    in_specs = [tok_spec(D), const_spec((1, D)), rope_spec, rope_spec,
                const_spec((1, HEAD_W)), const_spec((1, HEAD_W))]
    in_specs += [const_spec(w.shape) for w in w_sections]

    def tok_out(width):
        return jax.ShapeDtypeStruct((T, width), BF16)

    def headT_out(nh):
        return jax.ShapeDtypeStruct((B, nh, HEAD_W, S), BF16)

    out_shape = [headT_out(A_HEADS), tok_out(1024), headT_out(A_HEADS), tok_out(1024),
                 headT_out(B_HEADS), tok_out(256), headT_out(B_KV_HEADS), tok_out(1024),
                 tok_out(1024), tok_out(1024)]
    out_specs = [headT_spec(A_HEADS), tok_spec(1024), headT_spec(A_HEADS), tok_spec(1024),
                 headT_spec(B_HEADS), tok_spec(256), headT_spec(B_KV_HEADS), tok_spec(1024),
                 tok_spec(1024), tok_spec(1024)]
    return pl.pallas_call(
        _in_proj_kernel,
        out_shape=out_shape,
        grid=(T // tm,),
        in_specs=in_specs,
        out_specs=out_specs,
        compiler_params=_compiler_params(("arbitrary",)),
        name="in_proj",
    )(x2, g_norm.reshape(1, D), cos_t, sin_t, q_norm_b.reshape(1, HEAD_W), k_norm_b.reshape(1, HEAD_W),
      *w_sections)


def _softmax_chunk_update(s, v_t, m_ref, l_ref, acc_ref):
    m_prev = m_ref[...]
    m_new = jnp.maximum(m_prev, jnp.max(s, axis=0, keepdims=True))
    alpha = jnp.exp(m_prev - m_new)
    p = jnp.exp(s - m_new)
    l_ref[...] = alpha * l_ref[...] + jnp.sum(p, axis=0, keepdims=True)
    acc_ref[...] = alpha * acc_ref[...] + jnp.dot(v_t, p.astype(BF16), preferred_element_type=F32)
    m_ref[...] = m_new


def _attn_a_kernel(qT_ref, k_ref, vT_ref, bias_ref, lq1_ref, lk1_ref, lq2_ref, lk2_ref, subln_ref,
                   o_ref, m1_ref, l1_ref, acc1_ref, m2_ref, l2_ref, acc2_ref, *, tq, tk, lambda_init):
    i = pl.program_id(2)
    S = k_ref.shape[0]
    u = min(tq, tk)
    qT = qT_ref[0, 0]
    first_map = lax.broadcasted_iota(jnp.int32, qT.shape, 0) < A_QK_DIM
    zero = jnp.zeros_like(qT)
    q1 = jnp.where(first_map, qT, zero)
    q2 = jnp.where(first_map, zero, qT)
    for m_ref, l_ref, acc_ref in ((m1_ref, l1_ref, acc1_ref), (m2_ref, l2_ref, acc2_ref)):
        m_ref[...] = jnp.full(m_ref.shape, -jnp.inf, F32)
        l_ref[...] = jnp.zeros(l_ref.shape, F32)
        acc_ref[...] = jnp.zeros(acc_ref.shape, F32)

    def body(j, carry):
        start = pl.multiple_of(j * tk, tk)
        k_t = k_ref[pl.ds(start, tk), :]
        v_t = vT_ref[0, 0, :, pl.ds(start, tk)]
        offset = (j * tk - i * tq) // u
        tile = jnp.clip(offset, -(tk // u) - 1, tq // u + 1) + (tk // u + 1)
        bias = bias_ref[0, tile]
        s1 = jnp.dot(k_t, q1, preferred_element_type=F32) + bias
        _softmax_chunk_update(s1, v_t, m1_ref, l1_ref, acc1_ref)
        s2 = jnp.dot(k_t, q2, preferred_element_type=F32) + bias
        _softmax_chunk_update(s2, v_t, m2_ref, l2_ref, acc2_ref)
        return carry

    lax.fori_loop(0, S // tk, body, 0)

    lam = (jnp.exp(jnp.sum(lq1_ref[...] * lk1_ref[...], axis=-1, keepdims=True))
           - jnp.exp(jnp.sum(lq2_ref[...] * lk2_ref[...], axis=-1, keepdims=True))
           + lambda_init)
    oT = acc1_ref[...] / l1_ref[...] - lam * (acc2_ref[...] / l2_ref[...])
    ms = jnp.mean(oT * oT, axis=0, keepdims=True)
    oT = oT * lax.rsqrt(ms + EPS)
    o_ref[...] = (oT.T * (subln_ref[...] * (1.0 - lambda_init))).astype(BF16)


def _attn_a(qaT, ka, vaT, bias_tiles, lambdas, subln_w, lambda_init, tq, tk):
    B, H, _, S = qaT.shape
    nt = bias_tiles.shape[1]
    vec = pl.BlockSpec((1, A_QK_DIM), lambda b, h, i: (0, 0))
    kernel = functools.partial(_attn_a_kernel, tq=tq, tk=tk, lambda_init=lambda_init)
    return pl.pallas_call(
        kernel,
        out_shape=jax.ShapeDtypeStruct((B * S, H * HEAD_W), BF16),
        grid=(B, H, S // tq),
        in_specs=[
            pl.BlockSpec((1, 1, HEAD_W, tq), lambda b, h, i: (b, h, 0, i)),
            pl.BlockSpec((S, HEAD_W), lambda b, h, i: (b, h)),
            pl.BlockSpec((1, 1, HEAD_W, S), lambda b, h, i: (b, h, 0, 0)),
            pl.BlockSpec((1, nt, tk, tq), lambda b, h, i: (h, 0, 0, 0)),
            vec, vec, vec, vec,
            pl.BlockSpec((1, A_V_DIM), lambda b, h, i: (0, 0)),
        ],
        out_specs=pl.BlockSpec((tq, HEAD_W), lambda b, h, i: (b * (S // tq) + i, h)),
        scratch_shapes=[
            pltpu.VMEM((1, tq), F32), pltpu.VMEM((1, tq), F32), pltpu.VMEM((A_V_DIM, tq), F32),
            pltpu.VMEM((1, tq), F32), pltpu.VMEM((1, tq), F32), pltpu.VMEM((A_V_DIM, tq), F32),
        ],
        compiler_params=_compiler_params(("arbitrary", "arbitrary", "arbitrary")),
        name="attn_a",
    )(qaT, ka, vaT, bias_tiles, *lambdas, subln_w)


def _attn_b_kernel(qT_ref, k_ref, vT_ref, o_ref, m_ref, l_ref, acc_ref, *, tk):
    S = k_ref.shape[0]
    qT = qT_ref[0, 0]
    m_ref[...] = jnp.full(m_ref.shape, -jnp.inf, F32)
    l_ref[...] = jnp.zeros(l_ref.shape, F32)
    acc_ref[...] = jnp.zeros(acc_ref.shape, F32)

    def body(j, carry):
        start = pl.multiple_of(j * tk, tk)
        k_t = k_ref[pl.ds(start, tk), :]
        v_t = vT_ref[0, 0, :, pl.ds(start, tk)]
        s = jnp.dot(k_t, qT, preferred_element_type=F32)
        _softmax_chunk_update(s, v_t, m_ref, l_ref, acc_ref)
        return carry

    lax.fori_loop(0, S // tk, body, 0)
    oT = acc_ref[...] / l_ref[...]
    o_ref[...] = oT.T.astype(BF16)


def _attn_b(qbT, kb, vbT, tq, tk):
    B, H, _, S = qbT.shape
    kernel = functools.partial(_attn_b_kernel, tk=tk)
    return pl.pallas_call(
        kernel,
        out_shape=jax.ShapeDtypeStruct((B * S, H * HEAD_W), BF16),
        grid=(B, H, S // tq),
        in_specs=[
            pl.BlockSpec((1, 1, HEAD_W, tq), lambda b, h, i: (b, h, 0, i)),
            pl.BlockSpec((S, HEAD_W), lambda b, h, i: (b, h // B_GROUP)),
            pl.BlockSpec((1, 1, HEAD_W, S), lambda b, h, i: (b, h // B_GROUP, 0, 0)),
        ],
        out_specs=pl.BlockSpec((tq, HEAD_W), lambda b, h, i: (b * (S // tq) + i, h)),
        scratch_shapes=[pltpu.VMEM((1, tq), F32), pltpu.VMEM((1, tq), F32),
                        pltpu.VMEM((B_HEAD_DIM, tq), F32)],
        compiler_params=_compiler_params(("arbitrary", "arbitrary", "arbitrary")),
        name="attn_b",
    )(qbT, kb, vbT)


def _out_proj_kernel(x_ref, oa_ref, za_ref, ob_ref, zb_ref, ga_ref, gb_ref, wpa_ref, wpb_ref, wo_ref,
                     gf_ref, y_ref):
    ya = jnp.dot(oa_ref[...] * za_ref[...], wpa_ref[...], preferred_element_type=F32)
    yb = jnp.dot(ob_ref[...] * zb_ref[...], wpb_ref[...], preferred_element_type=F32)
    merged = ga_ref[...].astype(F32) * ya + gb_ref[...].astype(F32) * yb
    h = x_ref[...] + jnp.dot(merged.astype(BF16), wo_ref[...], preferred_element_type=F32)
    ms = jnp.mean(h * h, axis=-1, keepdims=True)
    y_ref[...] = h * lax.rsqrt(ms + EPS) * gf_ref[...]


def _out_proj(x2, oa, za, ob, zb, ga, gb, w_proj_a, w_proj_b, w_out, g_final, tm):
    T, D = x2.shape
    tok = pl.BlockSpec((tm, D), lambda t: (t, 0))

    def const_spec(shape):
        return pl.BlockSpec(shape, lambda t: (0,) * len(shape), pipeline_mode=pl.Buffered(1))

    return pl.pallas_call(
        _out_proj_kernel,
        out_shape=jax.ShapeDtypeStruct((T, D), F32),
        grid=(T // tm,),
        in_specs=[tok] * 7 + [const_spec((D, D))] * 3 + [const_spec((1, D))],
        out_specs=tok,
        compiler_params=_compiler_params(("arbitrary",)),
        name="out_proj",
    )(x2, oa, za, ob, zb, ga, gb, w_proj_a, w_proj_b, w_out, g_final.reshape(1, D))


IN_PROJ_ROWS = 256
OUT_PROJ_ROWS = 512
ATTN_A_TQ = 256
ATTN_A_TK = 256
ATTN_B_TQ = 256
ATTN_B_TK = 512


def _layer(x, l, g_norm, w_in, lambdas, subln_w, q_norm_b, k_norm_b, w_proj_a, w_proj_b, w_out,
           bias_tiles, g_final):
    B, S, D = x.shape
    offs = [0]
    for n in IN_SIZES:
        offs.append(offs[-1] + n)
    w_bf = w_in.astype(BF16)
    w_sections = [w_bf[:, offs[k]:offs[k + 1]] for k in range(len(IN_SIZES))]
    qaT, ka, vaT, za, qbT, kb, vbT, zb, ga, gb = _in_proj(x, g_norm, w_sections, q_norm_b, k_norm_b,
                                                          IN_PROJ_ROWS)
    lambda_init = 0.8 - 0.6 * math.exp(-0.3 * l)
    oa = _attn_a(qaT, ka, vaT, bias_tiles, [v.reshape(1, A_QK_DIM).astype(F32) for v in lambdas],
                 subln_w.reshape(1, A_V_DIM).astype(F32), lambda_init, ATTN_A_TQ, ATTN_A_TK)
    ob = _attn_b(qbT, kb, vbT, ATTN_B_TQ, ATTN_B_TK)
    y = _out_proj(x.reshape(B * S, D), oa, za, ob, zb, ga, gb, w_proj_a.astype(BF16), w_proj_b.astype(BF16),
                  w_out.astype(BF16), g_final, OUT_PROJ_ROWS)
    return y.reshape(B, S, D)


def kernel(x_prompt, x_sample, g_norm, w_in, lambda_q1, lambda_k1, lambda_q2, lambda_k2, subln_w,
           q_norm_b, k_norm_b, w_proj_a, w_proj_b, w_out, rel_bias, g_final):
    assert DEPTH == 1 and g_norm.shape[0] == DEPTH
    bias_tiles = _bias_tiles(rel_bias, ATTN_A_TQ, ATTN_A_TK)
    l = 0
    outs = []
    for x in (x_prompt, x_sample):
        outs.append(_layer(x, l, g_norm[l], w_in[l],
                           (lambda_q1[l], lambda_k1[l], lambda_q2[l], lambda_k2[l]), subln_w[l],
                           q_norm_b[l], k_norm_b[l], w_proj_a[l], w_proj_b[l], w_out[l], bias_tiles, g_final))
    return tuple(outs)
```

```python
import functools
import math

import jax
import jax.numpy as jnp
from jax import lax
from jax.experimental import pallas as pl
from jax.experimental.pallas import tpu as pltpu

D_MODEL = 1024
A_HEADS = 8
A_QK_DIM = 64
A_V_DIM = 128
B_HEADS = 8
B_KV_HEADS = 2
B_GROUP = B_HEADS // B_KV_HEADS
B_HEAD_DIM = 128
HEAD_W = 128
V_ONES_ROWS = 16
V_EXT_ROWS = HEAD_W + V_ONES_ROWS
ROPE_AXIS_DIM = 64
ROPE_THETA = 10000.0
GRID_W = 64
REL_BUCKETS = 32
REL_MAX_DIST = 128
EPS = 1e-6
DEPTH = 1
IN_SIZES = (1024, 1024, 1024, 1024, 1024, 256, 256, 1024, 1024, 1024)

BIAS_SATURATION_DIST = 128

VMEM_LIMIT_BYTES = 56 * 1024 * 1024

LOG2_E = math.log2(math.e)

F32 = jnp.float32
BF16 = jnp.bfloat16


def _compiler_params(semantics):
    return pltpu.CompilerParams(dimension_semantics=semantics, vmem_limit_bytes=VMEM_LIMIT_BYTES)


def _rel_bucket(rel):
    half = REL_BUCKETS // 2
    max_exact = half // 2
    ret = (rel > 0).astype(jnp.int32) * half
    n = jnp.abs(rel)
    nf = jnp.maximum(n, 1).astype(F32)
    large = max_exact + (jnp.log(nf / max_exact) / math.log(REL_MAX_DIST / max_exact)
                         * (half - max_exact)).astype(jnp.int32)
    large = jnp.minimum(large, half - 1)
    return ret + jnp.where(n < max_exact, n, large)


def _num_bias_tiles(tq, tk):
    u = min(tq, tk)
    return tk // u + tq // u + 3


def _bias_bucket_tiles(tq, tk):
    u = min(tq, tk)
    assert tq % u == 0 and tk % u == 0 and u >= BIAS_SATURATION_DIST
    nt = _num_bias_tiles(tq, tk)
    o = (jnp.arange(nt, dtype=jnp.int32) - (tk // u + 1)) * u
    kk = jnp.arange(tk, dtype=jnp.int32)
    qq = jnp.arange(tq, dtype=jnp.int32)
    rel = o[:, None, None] + kk[None, :, None] - qq[None, None, :]
    return _rel_bucket(rel)


def _bias_tiles_kernel(bucket_ref, rel_bias_ref, out_ref):
    h = pl.program_id(0)
    bucket = bucket_ref[0]
    acc = jnp.zeros(bucket.shape, F32)
    for b in range(REL_BUCKETS):
        acc = jnp.where(bucket == b, rel_bias_ref[b, h], acc)
    out_ref[0, 0] = acc * LOG2_E


def _bias_tiles(rel_bias, tq, tk):
    buckets = _bias_bucket_tiles(tq, tk)
    nt = buckets.shape[0]
    return pl.pallas_call(
        _bias_tiles_kernel,
        out_shape=jax.ShapeDtypeStruct((A_HEADS, nt, tk, tq), F32),
        grid=(A_HEADS, nt),
        in_specs=[
            pl.BlockSpec((1, tk, tq), lambda h, t: (t, 0, 0)),
            pl.BlockSpec(memory_space=pltpu.SMEM),
        ],
        out_specs=pl.BlockSpec((1, 1, tk, tq), lambda h, t: (h, t, 0, 0)),
        compiler_params=_compiler_params(("arbitrary", "arbitrary")),
        name="bias_tiles",
    )(buckets, rel_bias.astype(F32))


def _axial_tables(S):
    rows = S // GRID_W
    row_idx = jnp.repeat(jnp.arange(rows, dtype=F32), GRID_W)
    col_idx = jnp.tile(jnp.arange(GRID_W, dtype=F32), rows)
    inv_freq = ROPE_THETA ** (-jnp.arange(0, ROPE_AXIS_DIM, 2, dtype=F32) / ROPE_AXIS_DIM)
    ang_r = row_idx[:, None] * inv_freq[None, :]
    ang_c = col_idx[:, None] * inv_freq[None, :]
    cr, sr, cc, sc = jnp.cos(ang_r), jnp.sin(ang_r), jnp.cos(ang_c), jnp.sin(ang_c)
    cos_t = jnp.concatenate([cr, cr, cc, cc], axis=-1)
    sin_t = jnp.concatenate([-sr, sr, -sc, sc], axis=-1)
    return cos_t, sin_t


def _head_rms(t, gain):
    ms = jnp.mean(t * t, axis=-1, keepdims=True)
    return t * lax.rsqrt(ms + EPS) * gain


def _rope(t, cos_t, sin_t):
    lane = lax.broadcasted_iota(jnp.int32, t.shape, 1)
    quarter = ROPE_AXIS_DIM // 2
    partner = jnp.where((lane & quarter) == 0,
                        pltpu.roll(t, HEAD_W - quarter, 1), pltpu.roll(t, quarter, 1))
    return t * cos_t + partner * sin_t


def _in_proj_kernel(x_ref, g_ref, cos_ref, sin_ref, qn_ref, kn_ref,
                    w_qa, w_ka, w_va, w_za, w_qb, w_kb, w_vb, w_zb, w_ga, w_gb,
                    qaT_ref, ka_ref, vaT_ref, za_ref, qbT_ref, kb_ref, vbT_ref, zb_ref, ga_ref, gb_ref):
    x = x_ref[...]
    ms = jnp.mean(x * x, axis=-1, keepdims=True)
    xn = (x * lax.rsqrt(ms + EPS) * g_ref[...]).astype(BF16)

    def proj(w_ref):
        return jnp.dot(xn, w_ref[...], preferred_element_type=F32)

    def head(p, h):
        return p[:, h * HEAD_W:(h + 1) * HEAD_W]

    def store_vT(ref, h, t):
        ref[0, h, :HEAD_W] = t.T.astype(BF16)
        ref[0, h, HEAD_W:] = jnp.ones((V_ONES_ROWS, t.shape[0]), BF16)

    p = proj(w_qa) * (A_QK_DIM ** -0.5 * LOG2_E)
    first_map = lax.broadcasted_iota(jnp.int32, (HEAD_W, x.shape[0]), 0) < A_QK_DIM
    for h in range(A_HEADS):
        t = head(p, h).T
        qaT_ref[0, 2 * h] = jnp.where(first_map, t, 0.0).astype(BF16)
        qaT_ref[0, 2 * h + 1] = jnp.where(first_map, 0.0, t).astype(BF16)
    ka_ref[...] = proj(w_ka).astype(BF16)
    p = proj(w_va)
    for h in range(A_HEADS):
        store_vT(vaT_ref, h, head(p, h))
    p = proj(w_za)
    za_ref[...] = (p * jax.nn.sigmoid(p)).astype(BF16)

    cos_t = cos_ref[...]
    sin_t = sin_ref[...]
    p = proj(w_qb)
    for h in range(B_HEADS):
        t = _rope(_head_rms(head(p, h), qn_ref[...]), cos_t, sin_t) * (B_HEAD_DIM ** -0.5 * LOG2_E)
        qbT_ref[0, h] = t.T.astype(BF16)
    p = proj(w_kb)
    for h in range(B_KV_HEADS):
        t = _rope(_head_rms(head(p, h), kn_ref[...]), cos_t, sin_t)
        kb_ref[:, h * HEAD_W:(h + 1) * HEAD_W] = t.astype(BF16)
    p = proj(w_vb)
    for h in range(B_KV_HEADS):
        store_vT(vbT_ref, h, head(p, h))
    p = proj(w_zb)
    zb_ref[...] = (p * jax.nn.sigmoid(p)).astype(BF16)
    ga_ref[...] = jax.nn.sigmoid(proj(w_ga)).astype(BF16)
    gb_ref[...] = jax.nn.sigmoid(proj(w_gb)).astype(BF16)


def _in_proj(x, g_norm, w_sections, q_norm_b, k_norm_b, tm):
    B, S, D = x.shape
    T = B * S
    assert S % tm == 0
    tiles_per_seq = S // tm
    cos_t, sin_t = _axial_tables(S)
    x2 = x.reshape(T, D)

    def tok_spec(width):
        return pl.BlockSpec((tm, width), lambda t: (t, 0))

    def headT_spec(nh, rows=HEAD_W):
        return pl.BlockSpec((1, nh, rows, tm), lambda t: (t // tiles_per_seq, 0, 0, t % tiles_per_seq))

    def const_spec(shape):
        return pl.BlockSpec(shape, lambda t: (0,) * len(shape), pipeline_mode=pl.Buffered(1))

    rope_spec = pl.BlockSpec((tm, HEAD_W), lambda t: (t % tiles_per_seq, 0))
    in_specs = [tok_spec(D), const_spec((1, D)), rope_spec, rope_spec,
                const_spec((1, HEAD_W)), const_spec((1, HEAD_W))]
    in_specs += [const_spec(w.shape) for w in w_sections]

    def tok_out(width):
        return jax.ShapeDtypeStruct((T, width), BF16)

    def headT_out(nh, rows=HEAD_W):
        return jax.ShapeDtypeStruct((B, nh, rows, S), BF16)

    out_shape = [headT_out(2 * A_HEADS), tok_out(1024), headT_out(A_HEADS, V_EXT_ROWS), tok_out(1024),
                 headT_out(B_HEADS), tok_out(256), headT_out(B_KV_HEADS, V_EXT_ROWS), tok_out(1024),
                 tok_out(1024), tok_out(1024)]
    out_specs = [headT_spec(2 * A_HEADS), tok_spec(1024), headT_spec(A_HEADS, V_EXT_ROWS), tok_spec(1024),
                 headT_spec(B_HEADS), tok_spec(256), headT_spec(B_KV_HEADS, V_EXT_ROWS), tok_spec(1024),
                 tok_spec(1024), tok_spec(1024)]
    return pl.pallas_call(
        _in_proj_kernel,
        out_shape=out_shape,
        grid=(T // tm,),
        in_specs=in_specs,
        out_specs=out_specs,
        compiler_params=_compiler_params(("arbitrary",)),
        name="in_proj",
    )(x2, g_norm.reshape(1, D), cos_t, sin_t, q_norm_b.reshape(1, HEAD_W), k_norm_b.reshape(1, HEAD_W),
      *w_sections)


def _flash_scratch(nmaps, tq, tk, dv):
    per_map = [pltpu.VMEM((1, tq), F32), pltpu.VMEM((dv, tq), F32),
               pltpu.VMEM((tk, tq), F32), pltpu.VMEM((tk, tq), BF16), pltpu.VMEM((1, tq), F32),
               pltpu.VMEM((1, tq), F32)]
    return per_map * nmaps


def _flash_loop(qT_ref, k_ref, vT_ref, bias_fn, finalize_fn, scratch, *, nmaps, tq, tk):
    maps = [scratch[6 * mp:6 * mp + 6] for mp in range(nmaps)]
    S = k_ref.shape[0]
    n_chunks = S // tk
    n_items = (S // tq) * n_chunks
    assert n_chunks >= 2

    for _, acc_ref, _, _, _, _ in maps:
        acc_ref[...] = jnp.zeros(acc_ref.shape, F32)

    def scores(f):
        tile, c = f // n_chunks, f % n_chunks
        k_t = k_ref[pl.ds(pl.multiple_of(c * tk, tk), tk), :]
        bias = None if bias_fn is None else bias_fn(tile, c)
        for mp in range(nmaps):
            q_t = qT_ref[0, mp, :, pl.ds(pl.multiple_of(tile * tq, tq), tq)]
            s = jnp.dot(k_t, q_t, preferred_element_type=F32)
            if bias is not None:
                s = s + bias
            _, _, s_ref, _, _, smax_ref = maps[mp]
            s_ref[...] = s
            smax_ref[...] = jnp.max(s, axis=0, keepdims=True)

    def probs(f):
        first_chunk = (f % n_chunks) == 0
        for m_ref, _, s_ref, p_ref, alpha_ref, smax_ref in maps:
            m_prev = jnp.where(first_chunk, -jnp.inf, m_ref[...])
            m_new = jnp.maximum(m_prev, smax_ref[...])
            m_ref[...] = m_new
            alpha_ref[...] = jnp.exp2(m_prev - m_new)
            p_ref[...] = jnp.exp2(s_ref[...] - m_new).astype(BF16)

    def accumulate(f):
        c = f % n_chunks
        v_t = vT_ref[0, 0, :, pl.ds(pl.multiple_of(c * tk, tk), tk)]
        for _, acc_ref, _, p_ref, alpha_ref, _ in maps:
            acc_ref[...] = alpha_ref[...] * acc_ref[...] + jnp.dot(v_t, p_ref[...], preferred_element_type=F32)

    def finalize(tile):
        finalize_fn(tile, [acc_ref for _, acc_ref, _, _, _, _ in maps])

    scores(0)
    probs(0)
    scores(1)

    def body(f, carry):
        accumulate(f - 1)
        probs(f)
        scores(f + 1)

        @pl.when((f - 1) % n_chunks == n_chunks - 1)
        def _():
            finalize((f - 1) // n_chunks)

        return carry

    lax.fori_loop(1, n_items - 1, body, 0)
    accumulate(n_items - 2)
    probs(n_items - 1)
    accumulate(n_items - 1)
    finalize(S // tq - 1)


def _normalized(acc_ref):
    return acc_ref[:HEAD_W, :] * pl.reciprocal(acc_ref[HEAD_W:HEAD_W + 1, :])


def _attn_a_kernel(qT_ref, k_ref, vT_ref, bias_ref, lq1_ref, lk1_ref, lq2_ref, lk2_ref, subln_ref,
                   o_ref, *scratch, tq, tk, lambda_init):
    u = min(tq, tk)

    def bias_fn(tile, c):
        offset = (c * tk - tile * tq) // u
        idx = jnp.clip(offset, -(tk // u) - 1, tq // u + 1) + (tk // u + 1)
        return bias_ref[0, idx]

    def finalize_fn(tile, acc_refs):
        lam = (jnp.exp(jnp.sum(lq1_ref[...] * lk1_ref[...], axis=-1, keepdims=True))
               - jnp.exp(jnp.sum(lq2_ref[...] * lk2_ref[...], axis=-1, keepdims=True))
               + lambda_init)
        oT = _normalized(acc_refs[0]) - lam * _normalized(acc_refs[1])
        ms = jnp.mean(oT * oT, axis=0, keepdims=True)
        oT = oT * lax.rsqrt(ms + EPS)
        rows = pl.ds(pl.multiple_of(tile * tq, tq), tq)
        o_ref[rows, :] = (oT.T * (subln_ref[...] * (1.0 - lambda_init))).astype(BF16)

    _flash_loop(qT_ref, k_ref, vT_ref, bias_fn, finalize_fn, scratch, nmaps=2, tq=tq, tk=tk)


def _attn_a(qaT, ka, vaT, bias_tiles, lambdas, subln_w, lambda_init, tq, tk):
    B, H, _, S = vaT.shape
    nt = bias_tiles.shape[1]
    vec = pl.BlockSpec((1, A_QK_DIM), lambda b, h: (0, 0))
    kernel = functools.partial(_attn_a_kernel, tq=tq, tk=tk, lambda_init=lambda_init)
    return pl.pallas_call(
        kernel,
        out_shape=jax.ShapeDtypeStruct((B * S, H * HEAD_W), BF16),
        grid=(B, H),
        in_specs=[
            pl.BlockSpec((1, 2, HEAD_W, S), lambda b, h: (b, h, 0, 0)),
            pl.BlockSpec((S, HEAD_W), lambda b, h: (b, h)),
            pl.BlockSpec((1, 1, V_EXT_ROWS, S), lambda b, h: (b, h, 0, 0)),
            pl.BlockSpec((1, nt, tk, tq), lambda b, h: (h, 0, 0, 0), pipeline_mode=pl.Buffered(1)),
            vec, vec, vec, vec,
            pl.BlockSpec((1, A_V_DIM), lambda b, h: (0, 0)),
        ],
        out_specs=pl.BlockSpec((S, HEAD_W), lambda b, h: (b, h)),
        scratch_shapes=_flash_scratch(2, tq, tk, V_EXT_ROWS),
        compiler_params=_compiler_params(("arbitrary", "arbitrary")),
        name="attn_a",
    )(qaT, ka, vaT, bias_tiles, *lambdas, subln_w)


def _attn_b_kernel(qT_ref, k_ref, vT_ref, o_ref, *scratch, tq, tk):
    def finalize_fn(tile, acc_refs):
        rows = pl.ds(pl.multiple_of(tile * tq, tq), tq)
        o_ref[rows, :] = _normalized(acc_refs[0]).T.astype(BF16)

    _flash_loop(qT_ref, k_ref, vT_ref, None, finalize_fn, scratch, nmaps=1, tq=tq, tk=tk)


def _attn_b(qbT, kb, vbT, tq, tk):
    B, H, _, S = qbT.shape
    kernel = functools.partial(_attn_b_kernel, tq=tq, tk=tk)
    return pl.pallas_call(
        kernel,
        out_shape=jax.ShapeDtypeStruct((B * S, H * HEAD_W), BF16),
        grid=(B, H),
        in_specs=[
            pl.BlockSpec((1, 1, HEAD_W, S), lambda b, h: (b, h, 0, 0)),
            pl.BlockSpec((S, HEAD_W), lambda b, h: (b, h // B_GROUP)),
            pl.BlockSpec((1, 1, V_EXT_ROWS, S), lambda b, h: (b, h // B_GROUP, 0, 0)),
        ],
        out_specs=pl.BlockSpec((S, HEAD_W), lambda b, h: (b, h)),
        scratch_shapes=_flash_scratch(1, tq, tk, V_EXT_ROWS),
        compiler_params=_compiler_params(("arbitrary", "arbitrary")),
        name="attn_b",
    )(qbT, kb, vbT)


def _out_proj_kernel(x_ref, oa_ref, za_ref, ob_ref, zb_ref, ga_ref, gb_ref, wpa_ref, wpb_ref, wo_ref,
                     gf_ref, y_ref):
    ya = jnp.dot(oa_ref[...] * za_ref[...], wpa_ref[...], preferred_element_type=F32)
    yb = jnp.dot(ob_ref[...] * zb_ref[...], wpb_ref[...], preferred_element_type=F32)
    merged = ga_ref[...].astype(F32) * ya + gb_ref[...].astype(F32) * yb
    h = x_ref[...] + jnp.dot(merged.astype(BF16), wo_ref[...], preferred_element_type=F32)
    ms = jnp.mean(h * h, axis=-1, keepdims=True)
    y_ref[...] = h * lax.rsqrt(ms + EPS) * gf_ref[...]


def _out_proj(x2, oa, za, ob, zb, ga, gb, w_proj_a, w_proj_b, w_out, g_final, tm):
    T, D = x2.shape
    tok = pl.BlockSpec((tm, D), lambda t: (t, 0))

    def const_spec(shape):
        return pl.BlockSpec(shape, lambda t: (0,) * len(shape), pipeline_mode=pl.Buffered(1))

    return pl.pallas_call(
        _out_proj_kernel,
        out_shape=jax.ShapeDtypeStruct((T, D), F32),
        grid=(T // tm,),
        in_specs=[tok] * 7 + [const_spec((D, D))] * 3 + [const_spec((1, D))],
        out_specs=tok,
        compiler_params=_compiler_params(("arbitrary",)),
        name="out_proj",
    )(x2, oa, za, ob, zb, ga, gb, w_proj_a, w_proj_b, w_out, g_final.reshape(1, D))


IN_PROJ_ROWS = 256
OUT_PROJ_ROWS = 512
ATTN_A_TQ = 512
ATTN_A_TK = 1024
ATTN_B_TQ = 512
ATTN_B_TK = 2048


def _layer(x, l, g_norm, w_in, lambdas, subln_w, q_norm_b, k_norm_b, w_proj_a, w_proj_b, w_out,
           bias_tiles, g_final):
    B, S, D = x.shape
    offs = [0]
    for n in IN_SIZES:
        offs.append(offs[-1] + n)
    w_bf = w_in.astype(BF16)
    w_sections = [w_bf[:, offs[k]:offs[k + 1]] for k in range(len(IN_SIZES))]
    qaT, ka, vaT, za, qbT, kb, vbT, zb, ga, gb = _in_proj(x, g_norm, w_sections, q_norm_b, k_norm_b,
                                                          IN_PROJ_ROWS)
    lambda_init = 0.8 - 0.6 * math.exp(-0.3 * l)
    oa = _attn_a(qaT, ka, vaT, bias_tiles, [v.reshape(1, A_QK_DIM).astype(F32) for v in lambdas],
                 subln_w.reshape(1, A_V_DIM).astype(F32), lambda_init, ATTN_A_TQ, ATTN_A_TK)
    ob = _attn_b(qbT, kb, vbT, ATTN_B_TQ, ATTN_B_TK)
    y = _out_proj(x.reshape(B * S, D), oa, za, ob, zb, ga, gb, w_proj_a.astype(BF16), w_proj_b.astype(BF16),
                  w_out.astype(BF16), g_final, OUT_PROJ_ROWS)
    return y.reshape(B, S, D)


def kernel(x_prompt, x_sample, g_norm, w_in, lambda_q1, lambda_k1, lambda_q2, lambda_k2, subln_w,
           q_norm_b, k_norm_b, w_proj_a, w_proj_b, w_out, rel_bias, g_final):
    assert DEPTH == 1 and g_norm.shape[0] == DEPTH
    bias_tiles = _bias_tiles(rel_bias, ATTN_A_TQ, ATTN_A_TK)
    l = 0
    outs = []
    for x in (x_prompt, x_sample):
        outs.append(_layer(x, l, g_norm[l], w_in[l],
                           (lambda_q1[l], lambda_k1[l], lambda_q2[l], lambda_k2[l]), subln_w[l],
                           q_norm_b[l], k_norm_b[l], w_proj_a[l], w_proj_b[l], w_out[l], bias_tiles, g_final))
    return tuple(outs)
```

```python
import functools
import math

import jax
import jax.numpy as jnp
from jax import lax
from jax.experimental import pallas as pl
from jax.experimental.pallas import tpu as pltpu

D_MODEL = 1024
A_HEADS = 8
A_QK_DIM = 64
A_V_DIM = 128
B_HEADS = 8
B_KV_HEADS = 2
B_GROUP = B_HEADS // B_KV_HEADS
B_HEAD_DIM = 128
HEAD_W = 128
V_ONES_ROWS = 16
V_EXT_ROWS = HEAD_W + V_ONES_ROWS
ROPE_AXIS_DIM = 64
ROPE_THETA = 10000.0
GRID_W = 64
REL_BUCKETS = 32
REL_MAX_DIST = 128
EPS = 1e-6
DEPTH = 1
IN_SIZES = (1024, 1024, 1024, 1024, 1024, 256, 256, 1024, 1024, 1024)

BIAS_SATURATION_DIST = 128

VMEM_LIMIT_BYTES = 56 * 1024 * 1024

LOG2_E = math.log2(math.e)

F32 = jnp.float32
BF16 = jnp.bfloat16


def _compiler_params(semantics):
    return pltpu.CompilerParams(dimension_semantics=semantics, vmem_limit_bytes=VMEM_LIMIT_BYTES)


def _rel_bucket(rel):
    half = REL_BUCKETS // 2
    max_exact = half // 2
    ret = (rel > 0).astype(jnp.int32) * half
    n = jnp.abs(rel)
    nf = jnp.maximum(n, 1).astype(F32)
    large = max_exact + (jnp.log(nf / max_exact) / math.log(REL_MAX_DIST / max_exact)
                         * (half - max_exact)).astype(jnp.int32)
    large = jnp.minimum(large, half - 1)
    return ret + jnp.where(n < max_exact, n, large)


def _num_bias_tiles(tq, tk):
    u = min(tq, tk)
    return tk // u + tq // u + 3


def _bias_bucket_tiles(tq, tk):
    u = min(tq, tk)
    assert tq % u == 0 and tk % u == 0 and u >= BIAS_SATURATION_DIST
    nt = _num_bias_tiles(tq, tk)
    o = (jnp.arange(nt, dtype=jnp.int32) - (tk // u + 1)) * u
    kk = jnp.arange(tk, dtype=jnp.int32)
    qq = jnp.arange(tq, dtype=jnp.int32)
    rel = o[:, None, None] + kk[None, :, None] - qq[None, None, :]
    return _rel_bucket(rel)


def _bias_tiles_kernel(bucket_ref, rel_bias_ref, out_ref):
    h = pl.program_id(0)
    bucket = bucket_ref[0]
    acc = jnp.zeros(bucket.shape, F32)
    for b in range(REL_BUCKETS):
        acc = jnp.where(bucket == b, rel_bias_ref[b, h], acc)
    out_ref[0, 0] = acc * LOG2_E


def _bias_tiles(rel_bias, tq, tk):
    buckets = _bias_bucket_tiles(tq, tk)
    nt = buckets.shape[0]
    return pl.pallas_call(
        _bias_tiles_kernel,
        out_shape=jax.ShapeDtypeStruct((A_HEADS, nt, tk, tq), F32),
        grid=(A_HEADS, nt),
        in_specs=[
            pl.BlockSpec((1, tk, tq), lambda h, t: (t, 0, 0)),
            pl.BlockSpec(memory_space=pltpu.SMEM),
        ],
        out_specs=pl.BlockSpec((1, 1, tk, tq), lambda h, t: (h, t, 0, 0)),
        compiler_params=_compiler_params(("arbitrary", "arbitrary")),
        name="bias_tiles",
    )(buckets, rel_bias.astype(F32))


def _axial_tables(S):
    rows = S // GRID_W
    row_idx = jnp.repeat(jnp.arange(rows, dtype=F32), GRID_W)
    col_idx = jnp.tile(jnp.arange(GRID_W, dtype=F32), rows)
    inv_freq = ROPE_THETA ** (-jnp.arange(0, ROPE_AXIS_DIM, 2, dtype=F32) / ROPE_AXIS_DIM)
    ang_r = row_idx[:, None] * inv_freq[None, :]
    ang_c = col_idx[:, None] * inv_freq[None, :]
    cr, sr, cc, sc = jnp.cos(ang_r), jnp.sin(ang_r), jnp.cos(ang_c), jnp.sin(ang_c)
    cos_t = jnp.concatenate([cr, cr, cc, cc], axis=-1)
    sin_t = jnp.concatenate([-sr, sr, -sc, sc], axis=-1)
    return cos_t, sin_t


def _head_rms(t, gain):
    ms = jnp.mean(t * t, axis=-1, keepdims=True)
    return t * lax.rsqrt(ms + EPS) * gain


def _rope(t, cos_t, sin_t):
    lane = lax.broadcasted_iota(jnp.int32, t.shape, 1)
    quarter = ROPE_AXIS_DIM // 2
    partner = jnp.where((lane & quarter) == 0,
                        pltpu.roll(t, HEAD_W - quarter, 1), pltpu.roll(t, quarter, 1))
    return t * cos_t + partner * sin_t


def _in_proj_kernel(x_ref, g_ref, cos_ref, sin_ref, qn_ref, kn_ref, head_ones_ref,
                    w_qa, w_ka, w_va, w_za, w_qb, w_kb, w_vb, w_zb, w_ga, w_gb,
                    qaT_ref, ka_ref, vaT_ref, za_ref, qbT_ref, kb_ref, vbT_ref, zb_ref, ga_ref, gb_ref,
                    qa_sq_ref, ka_sq_ref):
    x = x_ref[...]
    ms = jnp.mean(x * x, axis=-1, keepdims=True)
    xn = (x * lax.rsqrt(ms + EPS) * g_ref[...]).astype(BF16)

    def proj(w_ref):
        return jnp.dot(xn, w_ref[...], preferred_element_type=F32)

    def head(p, h):
        return p[:, h * HEAD_W:(h + 1) * HEAD_W]

    def max_head_sq_norm(p):
        sq = jnp.dot((p * p).astype(BF16), head_ones_ref[...], preferred_element_type=F32)
        return jnp.max(sq.reshape(sq.shape[0] // 8, 8, HEAD_W), axis=0)

    def store_vT(ref, h, t):
        ref[0, h, :HEAD_W] = t.T.astype(BF16)
        ref[0, h, HEAD_W:] = jnp.ones((V_ONES_ROWS, t.shape[0]), BF16)

    p = proj(w_qa) * (A_QK_DIM ** -0.5 * LOG2_E)
    qa_sq_ref[0] = max_head_sq_norm(p)
    first_map = lax.broadcasted_iota(jnp.int32, (HEAD_W, x.shape[0]), 0) < A_QK_DIM
    for h in range(A_HEADS):
        t = head(p, h).T
        qaT_ref[0, 2 * h] = jnp.where(first_map, t, 0.0).astype(BF16)
        qaT_ref[0, 2 * h + 1] = jnp.where(first_map, 0.0, t).astype(BF16)
    p = proj(w_ka)
    ka_sq_ref[0] = max_head_sq_norm(p)
    ka_ref[...] = p.astype(BF16)
    p = proj(w_va)
    for h in range(A_HEADS):
        store_vT(vaT_ref, h, head(p, h))
    p = proj(w_za)
    za_ref[...] = (p * jax.nn.sigmoid(p)).astype(BF16)

    cos_t = cos_ref[...]
    sin_t = sin_ref[...]
    p = proj(w_qb)
    for h in range(B_HEADS):
        t = _rope(_head_rms(head(p, h), qn_ref[...]), cos_t, sin_t) * (B_HEAD_DIM ** -0.5 * LOG2_E)
        qbT_ref[0, h] = t.T.astype(BF16)
    p = proj(w_kb)
    for h in range(B_KV_HEADS):
        t = _rope(_head_rms(head(p, h), kn_ref[...]), cos_t, sin_t)
        kb_ref[:, h * HEAD_W:(h + 1) * HEAD_W] = t.astype(BF16)
    p = proj(w_vb)
    for h in range(B_KV_HEADS):
        store_vT(vbT_ref, h, head(p, h))
    p = proj(w_zb)
    zb_ref[...] = (p * jax.nn.sigmoid(p)).astype(BF16)
    ga_ref[...] = jax.nn.sigmoid(proj(w_ga)).astype(BF16)
    gb_ref[...] = jax.nn.sigmoid(proj(w_gb)).astype(BF16)


def _in_proj(x, g_norm, w_sections, q_norm_b, k_norm_b, tm):
    B, S, D = x.shape
    T = B * S
    assert S % tm == 0
    tiles_per_seq = S // tm
    cos_t, sin_t = _axial_tables(S)
    x2 = x.reshape(T, D)

    def tok_spec(width):
        return pl.BlockSpec((tm, width), lambda t: (t, 0))

    def headT_spec(nh, rows=HEAD_W):
        return pl.BlockSpec((1, nh, rows, tm), lambda t: (t // tiles_per_seq, 0, 0, t % tiles_per_seq))

    def const_spec(shape):
        return pl.BlockSpec(shape, lambda t: (0,) * len(shape), pipeline_mode=pl.Buffered(1))

    rope_spec = pl.BlockSpec((tm, HEAD_W), lambda t: (t % tiles_per_seq, 0))
    in_specs = [tok_spec(D), const_spec((1, D)), rope_spec, rope_spec,
                const_spec((1, HEAD_W)), const_spec((1, HEAD_W)), const_spec((A_HEADS * HEAD_W, HEAD_W))]
    in_specs += [const_spec(w.shape) for w in w_sections]
    head_ones = (jnp.arange(A_HEADS * HEAD_W)[:, None] // HEAD_W == jnp.arange(HEAD_W)[None, :]).astype(BF16)
    sq_out = jax.ShapeDtypeStruct((T // tm, 8, HEAD_W), F32)
    sq_spec = pl.BlockSpec((1, 8, HEAD_W), lambda t: (t, 0, 0))

    def tok_out(width):
        return jax.ShapeDtypeStruct((T, width), BF16)

    def headT_out(nh, rows=HEAD_W):
        return jax.ShapeDtypeStruct((B, nh, rows, S), BF16)

    out_shape = [headT_out(2 * A_HEADS), tok_out(1024), headT_out(A_HEADS, V_EXT_ROWS), tok_out(1024),
                 headT_out(B_HEADS), tok_out(256), headT_out(B_KV_HEADS, V_EXT_ROWS), tok_out(1024),
                 tok_out(1024), tok_out(1024), sq_out, sq_out]
    out_specs = [headT_spec(2 * A_HEADS), tok_spec(1024), headT_spec(A_HEADS, V_EXT_ROWS), tok_spec(1024),
                 headT_spec(B_HEADS), tok_spec(256), headT_spec(B_KV_HEADS, V_EXT_ROWS), tok_spec(1024),
                 tok_spec(1024), tok_spec(1024), sq_spec, sq_spec]
    return pl.pallas_call(
        _in_proj_kernel,
        out_shape=out_shape,
        grid=(T // tm,),
        in_specs=in_specs,
        out_specs=out_specs,
        compiler_params=_compiler_params(("arbitrary",)),
        name="in_proj",
    )(x2, g_norm.reshape(1, D), cos_t, sin_t, q_norm_b.reshape(1, HEAD_W), k_norm_b.reshape(1, HEAD_W),
      head_ones, *w_sections)


def _flash_scratch(nmaps, tq, tk, dv):
    per_map = [pltpu.VMEM((1, tq), F32), pltpu.VMEM((dv, tq), F32),
               pltpu.VMEM((tk, tq), F32), pltpu.VMEM((tk, tq), BF16), pltpu.VMEM((1, tq), F32),
               pltpu.VMEM((1, tq), F32)]
    return per_map * nmaps


def _flash_loop(qT_ref, k_ref, vT_ref, bias_fn, finalize_fn, scratch, *, nmaps, tq, tk):
    maps = [scratch[6 * mp:6 * mp + 6] for mp in range(nmaps)]
    S = k_ref.shape[0]
    n_chunks = S // tk
    n_items = (S // tq) * n_chunks
    assert n_chunks >= 2

    for _, acc_ref, _, _, _, _ in maps:
        acc_ref[...] = jnp.zeros(acc_ref.shape, F32)

    def scores(f):
        tile, c = f // n_chunks, f % n_chunks
        k_t = k_ref[pl.ds(pl.multiple_of(c * tk, tk), tk), :]
        bias = None if bias_fn is None else bias_fn(tile, c)
        for mp in range(nmaps):
            q_t = qT_ref[0, mp, :, pl.ds(pl.multiple_of(tile * tq, tq), tq)]
            s = jnp.dot(k_t, q_t, preferred_element_type=F32)
            if bias is not None:
                s = s + bias
            _, _, s_ref, _, _, smax_ref = maps[mp]
            s_ref[...] = s
            smax_ref[...] = jnp.max(s, axis=0, keepdims=True)

    def probs(f):
        first_chunk = (f % n_chunks) == 0
        for m_ref, _, s_ref, p_ref, alpha_ref, smax_ref in maps:
            m_prev = jnp.where(first_chunk, -jnp.inf, m_ref[...])
            m_new = jnp.maximum(m_prev, smax_ref[...])
            m_ref[...] = m_new
            alpha_ref[...] = jnp.exp2(m_prev - m_new)
            p_ref[...] = jnp.exp2(s_ref[...] - m_new).astype(BF16)

    def accumulate(f):
        c = f % n_chunks
        v_t = vT_ref[0, 0, :, pl.ds(pl.multiple_of(c * tk, tk), tk)]
        for _, acc_ref, _, p_ref, alpha_ref, _ in maps:
            acc_ref[...] = alpha_ref[...] * acc_ref[...] + jnp.dot(v_t, p_ref[...], preferred_element_type=F32)

    def finalize(tile):
        finalize_fn(tile, [acc_ref for _, acc_ref, _, _, _, _ in maps])

    scores(0)
    probs(0)
    scores(1)

    def body(f, carry):
        accumulate(f - 1)
        probs(f)
        scores(f + 1)

        @pl.when((f - 1) % n_chunks == n_chunks - 1)
        def _():
            finalize((f - 1) // n_chunks)

        return carry

    lax.fori_loop(1, n_items - 1, body, 0)
    accumulate(n_items - 2)
    probs(n_items - 1)
    accumulate(n_items - 1)
    finalize(S // tq - 1)


def _normalized(acc_ref):
    return acc_ref[:HEAD_W, :] * pl.reciprocal(acc_ref[HEAD_W:HEAD_W + 1, :])


def _diff_finalize(acc_refs, lambda_refs, subln_ref, o_ref, tile, tq, lambda_init):
    lq1_ref, lk1_ref, lq2_ref, lk2_ref = lambda_refs
    lam = (jnp.exp(jnp.sum(lq1_ref[...] * lk1_ref[...], axis=-1, keepdims=True))
           - jnp.exp(jnp.sum(lq2_ref[...] * lk2_ref[...], axis=-1, keepdims=True))
           + lambda_init)
    oT = _normalized(acc_refs[0]) - lam * _normalized(acc_refs[1])
    ms = jnp.mean(oT * oT, axis=0, keepdims=True)
    oT = oT * lax.rsqrt(ms + EPS)
    rows = pl.ds(pl.multiple_of(tile * tq, tq), tq)
    o_ref[rows, :] = (oT.T * (subln_ref[...] * (1.0 - lambda_init))).astype(BF16)


def _attn_a_kernel(qT_ref, k_ref, vT_ref, bias_ref, lq1_ref, lk1_ref, lq2_ref, lk2_ref, subln_ref,
                   o_ref, *scratch, tq, tk, lambda_init):
    u = min(tq, tk)

    def bias_fn(tile, c):
        offset = (c * tk - tile * tq) // u
        idx = jnp.clip(offset, -(tk // u) - 1, tq // u + 1) + (tk // u + 1)
        return bias_ref[0, idx]

    def finalize_fn(tile, acc_refs):
        _diff_finalize(acc_refs, (lq1_ref, lk1_ref, lq2_ref, lk2_ref), subln_ref, o_ref, tile, tq, lambda_init)

    _flash_loop(qT_ref, k_ref, vT_ref, bias_fn, finalize_fn, scratch, nmaps=2, tq=tq, tk=tk)


def _attn_a(qaT, ka, vaT, bias_tiles, lambdas, subln_w, lambda_init, tq, tk):
    B, H, _, S = vaT.shape
    nt = bias_tiles.shape[1]
    vec = pl.BlockSpec((1, A_QK_DIM), lambda b, h: (0, 0))
    kernel = functools.partial(_attn_a_kernel, tq=tq, tk=tk, lambda_init=lambda_init)
    return pl.pallas_call(
        kernel,
        out_shape=jax.ShapeDtypeStruct((B * S, H * HEAD_W), BF16),
        grid=(B, H),
        in_specs=[
            pl.BlockSpec((1, 2, HEAD_W, S), lambda b, h: (b, h, 0, 0)),
            pl.BlockSpec((S, HEAD_W), lambda b, h: (b, h)),
            pl.BlockSpec((1, 1, V_EXT_ROWS, S), lambda b, h: (b, h, 0, 0)),
            pl.BlockSpec((1, nt, tk, tq), lambda b, h: (h, 0, 0, 0), pipeline_mode=pl.Buffered(1)),
            vec, vec, vec, vec,
            pl.BlockSpec((1, A_V_DIM), lambda b, h: (0, 0)),
        ],
        out_specs=pl.BlockSpec((S, HEAD_W), lambda b, h: (b, h)),
        scratch_shapes=_flash_scratch(2, tq, tk, V_EXT_ROWS),
        compiler_params=_compiler_params(("arbitrary", "arbitrary")),
        name="attn_a",
    )(qaT, ka, vaT, bias_tiles, *lambdas, subln_w)


def _attn_b_kernel(qT_ref, k_ref, vT_ref, o_ref, *scratch, tq, tk):
    def finalize_fn(tile, acc_refs):
        rows = pl.ds(pl.multiple_of(tile * tq, tq), tq)
        o_ref[rows, :] = _normalized(acc_refs[0]).T.astype(BF16)

    _flash_loop(qT_ref, k_ref, vT_ref, None, finalize_fn, scratch, nmaps=1, tq=tq, tk=tk)


def _attn_b(qbT, kb, vbT, tq, tk):
    B, H, _, S = qbT.shape
    kernel = functools.partial(_attn_b_kernel, tq=tq, tk=tk)
    return pl.pallas_call(
        kernel,
        out_shape=jax.ShapeDtypeStruct((B * S, H * HEAD_W), BF16),
        grid=(B, H),
        in_specs=[
            pl.BlockSpec((1, 1, HEAD_W, S), lambda b, h: (b, h, 0, 0)),
            pl.BlockSpec((S, HEAD_W), lambda b, h: (b, h // B_GROUP)),
            pl.BlockSpec((1, 1, V_EXT_ROWS, S), lambda b, h: (b, h // B_GROUP, 0, 0)),
        ],
        out_specs=pl.BlockSpec((S, HEAD_W), lambda b, h: (b, h)),
        scratch_shapes=_flash_scratch(1, tq, tk, V_EXT_ROWS),
        compiler_params=_compiler_params(("arbitrary", "arbitrary")),
        name="attn_b",
    )(qbT, kb, vbT)


SCORE_BOUND = 60.0
NORM_MARGIN = 1.02
FAR_LEFT_BUCKET = REL_BUCKETS // 2 - 1
FAR_RIGHT_BUCKET = REL_BUCKETS - 1


def _bounded_scratch(nmaps, tq, tk):
    return [pltpu.VMEM((V_EXT_ROWS, tq), F32), pltpu.VMEM((tk, tq), BF16)] * nmaps


def _bounded_stages(qT_ref, k_ref, vT_ref, scratch, *, nmaps, tq, tk):
    maps = [scratch[2 * mp:2 * mp + 2] for mp in range(nmaps)]

    def init():
        for acc_ref, p_ref in maps:
            acc_ref[...] = jnp.zeros(acc_ref.shape, F32)
            p_ref[...] = jnp.zeros(p_ref.shape, BF16)

    def accumulate(c, keep):
        v_t = vT_ref[0, 0, :, pl.ds(pl.multiple_of(c * tk, tk), tk)]
        for acc_ref, p_ref in maps:
            acc_ref[...] = keep * acc_ref[...] + jnp.dot(v_t, p_ref[...], preferred_element_type=F32)

    def probs(tile, c, shift_fn):
        k_t = k_ref[pl.ds(pl.multiple_of(c * tk, tk), tk), :]
        for mp, (_, p_ref) in enumerate(maps):
            q_t = qT_ref[0, mp, :, pl.ds(pl.multiple_of(tile * tq, tq), tq)]
            s = jnp.dot(k_t, q_t, preferred_element_type=F32)
            p_ref[...] = jnp.exp2(shift_fn(s)).astype(BF16)

    return init, accumulate, probs, [acc_ref for acc_ref, _ in maps]


def _keep_unless_first(c):
    return jnp.where(c == 0, 0.0, 1.0).astype(F32)


def _attn_a_bounded_kernel(rel_bias_ref, qT_ref, k_ref, vT_ref, bias_ref, lq1_ref, lk1_ref, lq2_ref, lk2_ref,
                           subln_ref, o_ref, *scratch, tq, tk, lambda_init):
    S = k_ref.shape[0]
    n_chunks, n_tiles, u = S // tk, S // tq, min(tq, tk)
    assert n_chunks >= 2
    h = pl.program_id(0)
    far_left = rel_bias_ref[FAR_LEFT_BUCKET, h] * LOG2_E
    far_right = rel_bias_ref[FAR_RIGHT_BUCKET, h] * LOG2_E
    init, accumulate, probs, acc_refs = _bounded_stages(qT_ref, k_ref, vT_ref, scratch, nmaps=2, tq=tq, tk=tk)

    def finalize(tile):
        _diff_finalize(acc_refs, (lq1_ref, lk1_ref, lq2_ref, lk2_ref), subln_ref, o_ref, tile, tq, lambda_init)

    def step(tile, c, shift_fn, may_start_tile):
        c_prev = (c + n_chunks - 1) % n_chunks
        accumulate(c_prev, _keep_unless_first(c_prev))
        probs(tile, c, shift_fn)
        if may_start_tile:
            @pl.when((c == 0) & (tile > 0))
            def _():
                finalize(tile - 1)

    def tile_body(tile, carry):
        c_lo = jnp.maximum((tile * tq + tk - 1) // tk - 1, 0)
        c_hi = jnp.minimum((tile * tq + tq) // tk, n_chunks - 1)

        def far(const):
            return lambda s: s + const

        def near(c):
            idx = (c * tk - tile * tq + tk) // u + 1
            return lambda s: s + bias_ref[0, idx]

        def run(lo, hi, shift_of, may_start_tile):
            def body(c, carry):
                step(tile, c, shift_of(c), may_start_tile)
                return carry
            lax.fori_loop(lo, hi, body, 0)

        run(0, c_lo, lambda c: far(far_left), True)
        run(c_lo, c_hi + 1, near, True)
        run(c_hi + 1, n_chunks, lambda c: far(far_right), False)
        return carry

    init()
    lax.fori_loop(0, n_tiles, tile_body, 0)
    accumulate(n_chunks - 1, jnp.float32(1.0))
    finalize(n_tiles - 1)


def _attn_a_bounded(qaT, ka, vaT, bias_tiles, rel_bias, lambdas, subln_w, lambda_init, tq, tk):
    B, H, _, S = vaT.shape
    nt = bias_tiles.shape[1]
    vec = pl.BlockSpec((1, A_QK_DIM), lambda h, b: (0, 0))
    kernel = functools.partial(_attn_a_bounded_kernel, tq=tq, tk=tk, lambda_init=lambda_init)
    return pl.pallas_call(
        kernel,
        out_shape=jax.ShapeDtypeStruct((B * S, H * HEAD_W), BF16),
        grid=(H, B),
        in_specs=[
            pl.BlockSpec(memory_space=pltpu.SMEM),
            pl.BlockSpec((1, 2, HEAD_W, S), lambda h, b: (b, h, 0, 0)),
            pl.BlockSpec((S, HEAD_W), lambda h, b: (b, h)),
            pl.BlockSpec((1, 1, V_EXT_ROWS, S), lambda h, b: (b, h, 0, 0)),
            pl.BlockSpec((1, nt, tk, tq), lambda h, b: (h, 0, 0, 0)),
            vec, vec, vec, vec,
            pl.BlockSpec((1, A_V_DIM), lambda h, b: (0, 0)),
        ],
        out_specs=pl.BlockSpec((S, HEAD_W), lambda h, b: (b, h)),
        scratch_shapes=_bounded_scratch(2, tq, tk),
        compiler_params=_compiler_params(("arbitrary", "arbitrary")),
        name="attn_a_bounded",
    )(rel_bias.astype(F32), qaT, ka, vaT, bias_tiles, *lambdas, subln_w)


def _attn_b_bounded_kernel(qT_ref, k_ref, vT_ref, o_ref, *scratch, tq, tk):
    S = k_ref.shape[0]
    n_chunks, n_tiles = S // tk, S // tq
    assert n_chunks >= 2
    init, accumulate, probs, acc_refs = _bounded_stages(qT_ref, k_ref, vT_ref, scratch, nmaps=1, tq=tq, tk=tk)

    def finalize(tile):
        rows = pl.ds(pl.multiple_of(tile * tq, tq), tq)
        o_ref[rows, :] = _normalized(acc_refs[0]).T.astype(BF16)

    def body(g, carry):
        tile, c = g // n_chunks, g % n_chunks
        c_prev = (c + n_chunks - 1) % n_chunks
        accumulate(c_prev, _keep_unless_first(c_prev))
        probs(tile, c, lambda s: s)

        @pl.when((c == 0) & (tile > 0))
        def _():
            finalize(tile - 1)

        return carry

    init()
    lax.fori_loop(0, n_tiles * n_chunks, body, 0)
    accumulate(n_chunks - 1, jnp.float32(1.0))
    finalize(n_tiles - 1)


def _attn_b_bounded(qbT, kb, vbT, tq, tk):
    B, H, _, S = qbT.shape
    kernel = functools.partial(_attn_b_bounded_kernel, tq=tq, tk=tk)
    return pl.pallas_call(
        kernel,
        out_shape=jax.ShapeDtypeStruct((B * S, H * HEAD_W), BF16),
        grid=(B, H),
        in_specs=[
            pl.BlockSpec((1, 1, HEAD_W, S), lambda b, h: (b, h, 0, 0)),
            pl.BlockSpec((S, HEAD_W), lambda b, h: (b, h // B_GROUP)),
            pl.BlockSpec((1, 1, V_EXT_ROWS, S), lambda b, h: (b, h // B_GROUP, 0, 0)),
        ],
        out_specs=pl.BlockSpec((S, HEAD_W), lambda b, h: (b, h)),
        scratch_shapes=_bounded_scratch(1, tq, tk),
        compiler_params=_compiler_params(("arbitrary", "arbitrary")),
        name="attn_b_bounded",
    )(qbT, kb, vbT)


def _out_proj_kernel(x_ref, oa_ref, za_ref, ob_ref, zb_ref, ga_ref, gb_ref, wpa_ref, wpb_ref, wo_ref,
                     gf_ref, y_ref):
    ya = jnp.dot(oa_ref[...] * za_ref[...], wpa_ref[...], preferred_element_type=F32)
    yb = jnp.dot(ob_ref[...] * zb_ref[...], wpb_ref[...], preferred_element_type=F32)
    merged = ga_ref[...].astype(F32) * ya + gb_ref[...].astype(F32) * yb
    h = x_ref[...] + jnp.dot(merged.astype(BF16), wo_ref[...], preferred_element_type=F32)
    ms = jnp.mean(h * h, axis=-1, keepdims=True)
    y_ref[...] = h * lax.rsqrt(ms + EPS) * gf_ref[...]


def _out_proj(x2, oa, za, ob, zb, ga, gb, w_proj_a, w_proj_b, w_out, g_final, tm):
    T, D = x2.shape
    tok = pl.BlockSpec((tm, D), lambda t: (t, 0))

    def const_spec(shape):
        return pl.BlockSpec(shape, lambda t: (0,) * len(shape), pipeline_mode=pl.Buffered(1))

    return pl.pallas_call(
        _out_proj_kernel,
        out_shape=jax.ShapeDtypeStruct((T, D), F32),
        grid=(T // tm,),
        in_specs=[tok] * 7 + [const_spec((D, D))] * 3 + [const_spec((1, D))],
        out_specs=tok,
        compiler_params=_compiler_params(("arbitrary",)),
        name="out_proj",
    )(x2, oa, za, ob, zb, ga, gb, w_proj_a, w_proj_b, w_out, g_final.reshape(1, D))


IN_PROJ_ROWS = 256
OUT_PROJ_ROWS = 512
ATTN_A_TQ = 512
ATTN_A_TK = 1024
ATTN_B_TQ = 512
ATTN_B_TK = 2048


def _layer(x, l, g_norm, w_in, lambdas, subln_w, q_norm_b, k_norm_b, w_proj_a, w_proj_b, w_out,
           bias_tiles, rel_bias, g_final):
    B, S, D = x.shape
    offs = [0]
    for n in IN_SIZES:
        offs.append(offs[-1] + n)
    w_bf = w_in.astype(BF16)
    w_sections = [w_bf[:, offs[k]:offs[k + 1]] for k in range(len(IN_SIZES))]
    qaT, ka, vaT, za, qbT, kb, vbT, zb, ga, gb, qa_sq, ka_sq = _in_proj(x, g_norm, w_sections, q_norm_b,
                                                                        k_norm_b, IN_PROJ_ROWS)
    lambda_init = 0.8 - 0.6 * math.exp(-0.3 * l)
    lambdas = [v.reshape(1, A_QK_DIM).astype(F32) for v in lambdas]
    subln = subln_w.reshape(1, A_V_DIM).astype(F32)

    bound_a = (jnp.sqrt(jnp.max(qa_sq) * jnp.max(ka_sq)) * NORM_MARGIN
               + jnp.max(jnp.abs(rel_bias)) * LOG2_E)
    oa = lax.cond(
        bound_a <= SCORE_BOUND,
        lambda: _attn_a_bounded(qaT, ka, vaT, bias_tiles, rel_bias, lambdas, subln, lambda_init,
                                ATTN_A_TQ, ATTN_A_TK),
        lambda: _attn_a(qaT, ka, vaT, bias_tiles, lambdas, subln, lambda_init, ATTN_A_TQ, ATTN_A_TK))
    bound_b = (B_HEAD_DIM * jnp.max(jnp.abs(q_norm_b)) * jnp.max(jnp.abs(k_norm_b))
               * (B_HEAD_DIM ** -0.5 * LOG2_E) * NORM_MARGIN)
    ob = lax.cond(
        bound_b <= SCORE_BOUND,
        lambda: _attn_b_bounded(qbT, kb, vbT, ATTN_B_TQ, ATTN_B_TK),
        lambda: _attn_b(qbT, kb, vbT, ATTN_B_TQ, ATTN_B_TK))
    y = _out_proj(x.reshape(B * S, D), oa, za, ob, zb, ga, gb, w_proj_a.astype(BF16), w_proj_b.astype(BF16),
                  w_out.astype(BF16), g_final, OUT_PROJ_ROWS)
    return y.reshape(B, S, D)


def kernel(x_prompt, x_sample, g_norm, w_in, lambda_q1, lambda_k1, lambda_q2, lambda_k2, subln_w,
           q_norm_b, k_norm_b, w_proj_a, w_proj_b, w_out, rel_bias, g_final):
    assert DEPTH == 1 and g_norm.shape[0] == DEPTH
    bias_tiles = _bias_tiles(rel_bias, ATTN_A_TQ, ATTN_A_TK)
    l = 0
    outs = []
    for x in (x_prompt, x_sample):
        outs.append(_layer(x, l, g_norm[l], w_in[l],
                           (lambda_q1[l], lambda_k1[l], lambda_q2[l], lambda_k2[l]), subln_w[l],
                           q_norm_b[l], k_norm_b[l], w_proj_a[l], w_proj_b[l], w_out[l], bias_tiles, rel_bias,
                           g_final))
    return tuple(outs)
```

```python
import functools
import math

import jax
import jax.numpy as jnp
from jax import lax
from jax.experimental import pallas as pl
from jax.experimental.pallas import tpu as pltpu

D_MODEL = 1024
A_HEADS = 8
A_QK_DIM = 64
A_V_DIM = 128
B_HEADS = 8
B_KV_HEADS = 2
B_GROUP = B_HEADS // B_KV_HEADS
B_HEAD_DIM = 128
HEAD_W = 128
V_ONES_ROWS = 16
V_EXT_ROWS = HEAD_W + V_ONES_ROWS
ROPE_AXIS_DIM = 64
ROPE_THETA = 10000.0
GRID_W = 64
REL_BUCKETS = 32
REL_MAX_DIST = 128
EPS = 1e-6
DEPTH = 1
IN_SIZES = (1024, 1024, 1024, 1024, 1024, 256, 256, 1024, 1024, 1024)

BIAS_SATURATION_DIST = 128

VMEM_LIMIT_BYTES = 56 * 1024 * 1024

LOG2_E = math.log2(math.e)

F32 = jnp.float32
BF16 = jnp.bfloat16


def _compiler_params(semantics):
    return pltpu.CompilerParams(dimension_semantics=semantics, vmem_limit_bytes=VMEM_LIMIT_BYTES)


def _rel_bucket(rel):
    half = REL_BUCKETS // 2
    max_exact = half // 2
    ret = (rel > 0).astype(jnp.int32) * half
    n = jnp.abs(rel)
    nf = jnp.maximum(n, 1).astype(F32)
    large = max_exact + (jnp.log(nf / max_exact) / math.log(REL_MAX_DIST / max_exact)
                         * (half - max_exact)).astype(jnp.int32)
    large = jnp.minimum(large, half - 1)
    return ret + jnp.where(n < max_exact, n, large)


def _num_bias_tiles(tq, tk):
    u = min(tq, tk)
    return tk // u + tq // u + 3


def _bias_bucket_tiles(tq, tk):
    u = min(tq, tk)
    assert tq % u == 0 and tk % u == 0 and u >= BIAS_SATURATION_DIST
    nt = _num_bias_tiles(tq, tk)
    o = (jnp.arange(nt, dtype=jnp.int32) - (tk // u + 1)) * u
    kk = jnp.arange(tk, dtype=jnp.int32)
    qq = jnp.arange(tq, dtype=jnp.int32)
    rel = o[:, None, None] + kk[None, :, None] - qq[None, None, :]
    return _rel_bucket(rel)


def _bias_tiles_kernel(bucket_ref, rel_bias_ref, out_ref):
    h = pl.program_id(0)
    bucket = bucket_ref[0]
    acc = jnp.zeros(bucket.shape, F32)
    for b in range(REL_BUCKETS):
        acc = jnp.where(bucket == b, rel_bias_ref[b, h], acc)
    out_ref[0, 0] = acc * LOG2_E


def _bias_tiles(rel_bias, tq, tk):
    buckets = _bias_bucket_tiles(tq, tk)
    nt = buckets.shape[0]
    return pl.pallas_call(
        _bias_tiles_kernel,
        out_shape=jax.ShapeDtypeStruct((A_HEADS, nt, tk, tq), F32),
        grid=(A_HEADS, nt),
        in_specs=[
            pl.BlockSpec((1, tk, tq), lambda h, t: (t, 0, 0)),
            pl.BlockSpec(memory_space=pltpu.SMEM),
        ],
        out_specs=pl.BlockSpec((1, 1, tk, tq), lambda h, t: (h, t, 0, 0)),
        compiler_params=_compiler_params(("arbitrary", "arbitrary")),
        name="bias_tiles",
    )(buckets, rel_bias.astype(F32))


def _axial_tables(S):
    rows = S // GRID_W
    row_idx = jnp.repeat(jnp.arange(rows, dtype=F32), GRID_W)
    col_idx = jnp.tile(jnp.arange(GRID_W, dtype=F32), rows)
    inv_freq = ROPE_THETA ** (-jnp.arange(0, ROPE_AXIS_DIM, 2, dtype=F32) / ROPE_AXIS_DIM)
    ang_r = row_idx[:, None] * inv_freq[None, :]
    ang_c = col_idx[:, None] * inv_freq[None, :]
    cr, sr, cc, sc = jnp.cos(ang_r), jnp.sin(ang_r), jnp.cos(ang_c), jnp.sin(ang_c)
    cos_t = jnp.concatenate([cr, cr, cc, cc], axis=-1)
    sin_t = jnp.concatenate([-sr, sr, -sc, sc], axis=-1)
    return cos_t, sin_t


def _head_rms(t, gain):
    ms = jnp.mean(t * t, axis=-1, keepdims=True)
    return t * lax.rsqrt(ms + EPS) * gain


def _rope(t, cos_t, sin_t):
    lane = lax.broadcasted_iota(jnp.int32, t.shape, 1)
    quarter = ROPE_AXIS_DIM // 2
    partner = jnp.where((lane & quarter) == 0,
                        pltpu.roll(t, HEAD_W - quarter, 1), pltpu.roll(t, quarter, 1))
    return t * cos_t + partner * sin_t


def _in_proj_kernel(x_ref, g_ref, cos_ref, sin_ref, qn_ref, kn_ref, head_ones_ref,
                    w_qa, w_ka, w_va, w_za, w_qb, w_kb, w_vb, w_zb, w_ga, w_gb,
                    qaT_ref, ka_ref, vaT_ref, za_ref, qbT_ref, kb_ref, vbT_ref, zb_ref, ga_ref, gb_ref,
                    qa_sq_ref, ka_sq_ref):
    x = x_ref[...]
    ms = jnp.mean(x * x, axis=-1, keepdims=True)
    xn = (x * lax.rsqrt(ms + EPS) * g_ref[...]).astype(BF16)

    def proj(w_ref):
        return jnp.dot(xn, w_ref[...], preferred_element_type=F32)

    def head(p, h):
        return p[:, h * HEAD_W:(h + 1) * HEAD_W]

    def max_head_sq_norm(p):
        sq = jnp.dot((p * p).astype(BF16), head_ones_ref[...], preferred_element_type=F32)
        return jnp.max(sq.reshape(sq.shape[0] // 8, 8, HEAD_W), axis=0)

    def store_vT(ref, h, t):
        ref[0, h, :HEAD_W] = t.T.astype(BF16)
        ref[0, h, HEAD_W:] = jnp.ones((V_ONES_ROWS, t.shape[0]), BF16)

    p = proj(w_qa) * (A_QK_DIM ** -0.5 * LOG2_E)
    qa_sq_ref[0] = max_head_sq_norm(p)
    first_map = lax.broadcasted_iota(jnp.int32, (HEAD_W, x.shape[0]), 0) < A_QK_DIM
    for h in range(A_HEADS):
        t = head(p, h).T
        qaT_ref[0, 2 * h] = jnp.where(first_map, t, 0.0).astype(BF16)
        qaT_ref[0, 2 * h + 1] = jnp.where(first_map, 0.0, t).astype(BF16)
    p = proj(w_ka)
    ka_sq_ref[0] = max_head_sq_norm(p)
    ka_ref[...] = p.astype(BF16)
    p = proj(w_va)
    for h in range(A_HEADS):
        store_vT(vaT_ref, h, head(p, h))
    p = proj(w_za)
    za_ref[...] = (p * jax.nn.sigmoid(p)).astype(BF16)

    cos_t = cos_ref[...]
    sin_t = sin_ref[...]
    p = proj(w_qb)
    for h in range(B_HEADS):
        t = _rope(_head_rms(head(p, h), qn_ref[...]), cos_t, sin_t) * (B_HEAD_DIM ** -0.5 * LOG2_E)
        qbT_ref[0, h] = t.T.astype(BF16)
    p = proj(w_kb)
    for h in range(B_KV_HEADS):
        t = _rope(_head_rms(head(p, h), kn_ref[...]), cos_t, sin_t)
        kb_ref[:, h * HEAD_W:(h + 1) * HEAD_W] = t.astype(BF16)
    p = proj(w_vb)
    for h in range(B_KV_HEADS):
        store_vT(vbT_ref, h, head(p, h))
    p = proj(w_zb)
    zb_ref[...] = (p * jax.nn.sigmoid(p)).astype(BF16)
    ga_ref[...] = jax.nn.sigmoid(proj(w_ga)).astype(BF16)
    gb_ref[...] = jax.nn.sigmoid(proj(w_gb)).astype(BF16)


def _in_proj(x, g_norm, w_sections, q_norm_b, k_norm_b, tm):
    B, S, D = x.shape
    T = B * S
    assert S % tm == 0
    tiles_per_seq = S // tm
    cos_t, sin_t = _axial_tables(S)
    x2 = x.reshape(T, D)

    def tok_spec(width):
        return pl.BlockSpec((tm, width), lambda t: (t, 0))

    def headT_spec(nh, rows=HEAD_W):
        return pl.BlockSpec((1, nh, rows, tm), lambda t: (t // tiles_per_seq, 0, 0, t % tiles_per_seq))

    def const_spec(shape):
        return pl.BlockSpec(shape, lambda t: (0,) * len(shape), pipeline_mode=pl.Buffered(1))

    rope_spec = pl.BlockSpec((tm, HEAD_W), lambda t: (t % tiles_per_seq, 0))
    in_specs = [tok_spec(D), const_spec((1, D)), rope_spec, rope_spec,
                const_spec((1, HEAD_W)), const_spec((1, HEAD_W)), const_spec((A_HEADS * HEAD_W, HEAD_W))]
    in_specs += [const_spec(w.shape) for w in w_sections]
    head_ones = (jnp.arange(A_HEADS * HEAD_W)[:, None] // HEAD_W == jnp.arange(HEAD_W)[None, :]).astype(BF16)
    sq_out = jax.ShapeDtypeStruct((T // tm, 8, HEAD_W), F32)
    sq_spec = pl.BlockSpec((1, 8, HEAD_W), lambda t: (t, 0, 0))

    def tok_out(width):
        return jax.ShapeDtypeStruct((T, width), BF16)

    def headT_out(nh, rows=HEAD_W):
        return jax.ShapeDtypeStruct((B, nh, rows, S), BF16)

    out_shape = [headT_out(2 * A_HEADS), tok_out(1024), headT_out(A_HEADS, V_EXT_ROWS), tok_out(1024),
                 headT_out(B_HEADS), tok_out(256), headT_out(B_KV_HEADS, V_EXT_ROWS), tok_out(1024),
                 tok_out(1024), tok_out(1024), sq_out, sq_out]
    out_specs = [headT_spec(2 * A_HEADS), tok_spec(1024), headT_spec(A_HEADS, V_EXT_ROWS), tok_spec(1024),
                 headT_spec(B_HEADS), tok_spec(256), headT_spec(B_KV_HEADS, V_EXT_ROWS), tok_spec(1024),
                 tok_spec(1024), tok_spec(1024), sq_spec, sq_spec]
    return pl.pallas_call(
        _in_proj_kernel,
        out_shape=out_shape,
        grid=(T // tm,),
        in_specs=in_specs,
        out_specs=out_specs,
        compiler_params=_compiler_params(("arbitrary",)),
        name="in_proj",
    )(x2, g_norm.reshape(1, D), cos_t, sin_t, q_norm_b.reshape(1, HEAD_W), k_norm_b.reshape(1, HEAD_W),
      head_ones, *w_sections)


def _flash_scratch(nmaps, tq, tk, dv):
    per_map = [pltpu.VMEM((1, tq), F32), pltpu.VMEM((dv, tq), F32),
               pltpu.VMEM((tk, tq), F32), pltpu.VMEM((tk, tq), BF16), pltpu.VMEM((1, tq), F32),
               pltpu.VMEM((1, tq), F32)]
    return per_map * nmaps


def _flash_loop(qT_ref, k_ref, vT_ref, bias_fn, finalize_fn, scratch, *, nmaps, tq, tk):
    maps = [scratch[6 * mp:6 * mp + 6] for mp in range(nmaps)]
    S = k_ref.shape[0]
    n_chunks = S // tk
    n_items = (S // tq) * n_chunks
    assert n_chunks >= 2

    for _, acc_ref, _, _, _, _ in maps:
        acc_ref[...] = jnp.zeros(acc_ref.shape, F32)

    def scores(f):
        tile, c = f // n_chunks, f % n_chunks
        k_t = k_ref[pl.ds(pl.multiple_of(c * tk, tk), tk), :]
        bias = None if bias_fn is None else bias_fn(tile, c)
        for mp in range(nmaps):
            q_t = qT_ref[0, mp, :, pl.ds(pl.multiple_of(tile * tq, tq), tq)]
            s = jnp.dot(k_t, q_t, preferred_element_type=F32)
            if bias is not None:
                s = s + bias
            _, _, s_ref, _, _, smax_ref = maps[mp]
            s_ref[...] = s
            smax_ref[...] = jnp.max(s, axis=0, keepdims=True)

    def probs(f):
        first_chunk = (f % n_chunks) == 0
        for m_ref, _, s_ref, p_ref, alpha_ref, smax_ref in maps:
            m_prev = jnp.where(first_chunk, -jnp.inf, m_ref[...])
            m_new = jnp.maximum(m_prev, smax_ref[...])
            m_ref[...] = m_new
            alpha_ref[...] = jnp.exp2(m_prev - m_new)
            p_ref[...] = jnp.exp2(s_ref[...] - m_new).astype(BF16)

    def accumulate(f):
        c = f % n_chunks
        v_t = vT_ref[0, 0, :, pl.ds(pl.multiple_of(c * tk, tk), tk)]
        for _, acc_ref, _, p_ref, alpha_ref, _ in maps:
            acc_ref[...] = alpha_ref[...] * acc_ref[...] + jnp.dot(v_t, p_ref[...], preferred_element_type=F32)

    def finalize(tile):
        finalize_fn(tile, [acc_ref for _, acc_ref, _, _, _, _ in maps])

    scores(0)
    probs(0)
    scores(1)

    def body(f, carry):
        accumulate(f - 1)
        probs(f)
        scores(f + 1)

        @pl.when((f - 1) % n_chunks == n_chunks - 1)
        def _():
            finalize((f - 1) // n_chunks)

        return carry

    lax.fori_loop(1, n_items - 1, body, 0)
    accumulate(n_items - 2)
    probs(n_items - 1)
    accumulate(n_items - 1)
    finalize(S // tq - 1)


def _normalized(acc_ref):
    return acc_ref[:HEAD_W, :] * pl.reciprocal(acc_ref[HEAD_W:HEAD_W + 1, :])


def _diff_finalize(o1T, o2T, lambda_refs, subln_ref, o_ref, tile, tq, lambda_init):
    lq1_ref, lk1_ref, lq2_ref, lk2_ref = lambda_refs
    lam = (jnp.exp(jnp.sum(lq1_ref[...] * lk1_ref[...], axis=-1, keepdims=True))
           - jnp.exp(jnp.sum(lq2_ref[...] * lk2_ref[...], axis=-1, keepdims=True))
           + lambda_init)
    oT = o1T - lam * o2T
    ms = jnp.mean(oT * oT, axis=0, keepdims=True)
    oT = oT * lax.rsqrt(ms + EPS)
    rows = pl.ds(pl.multiple_of(tile * tq, tq), tq)
    o_ref[rows, :] = (oT.T * (subln_ref[...] * (1.0 - lambda_init))).astype(BF16)


def _attn_a_kernel(qT_ref, k_ref, vT_ref, bias_ref, lq1_ref, lk1_ref, lq2_ref, lk2_ref, subln_ref,
                   o_ref, *scratch, tq, tk, lambda_init):
    u = min(tq, tk)

    def bias_fn(tile, c):
        offset = (c * tk - tile * tq) // u
        idx = jnp.clip(offset, -(tk // u) - 1, tq // u + 1) + (tk // u + 1)
        return bias_ref[0, idx]

    def finalize_fn(tile, acc_refs):
        _diff_finalize(_normalized(acc_refs[0]), _normalized(acc_refs[1]),
                       (lq1_ref, lk1_ref, lq2_ref, lk2_ref), subln_ref, o_ref, tile, tq, lambda_init)

    _flash_loop(qT_ref, k_ref, vT_ref, bias_fn, finalize_fn, scratch, nmaps=2, tq=tq, tk=tk)


def _attn_a(qaT, ka, vaT, bias_tiles, lambdas, subln_w, lambda_init, tq, tk):
    B, H, _, S = vaT.shape
    nt = bias_tiles.shape[1]
    vec = pl.BlockSpec((1, A_QK_DIM), lambda b, h: (0, 0))
    kernel = functools.partial(_attn_a_kernel, tq=tq, tk=tk, lambda_init=lambda_init)
    return pl.pallas_call(
        kernel,
        out_shape=jax.ShapeDtypeStruct((B * S, H * HEAD_W), BF16),
        grid=(B, H),
        in_specs=[
            pl.BlockSpec((1, 2, HEAD_W, S), lambda b, h: (b, h, 0, 0)),
            pl.BlockSpec((S, HEAD_W), lambda b, h: (b, h)),
            pl.BlockSpec((1, 1, V_EXT_ROWS, S), lambda b, h: (b, h, 0, 0)),
            pl.BlockSpec((1, nt, tk, tq), lambda b, h: (h, 0, 0, 0), pipeline_mode=pl.Buffered(1)),
            vec, vec, vec, vec,
            pl.BlockSpec((1, A_V_DIM), lambda b, h: (0, 0)),
        ],
        out_specs=pl.BlockSpec((S, HEAD_W), lambda b, h: (b, h)),
        scratch_shapes=_flash_scratch(2, tq, tk, V_EXT_ROWS),
        compiler_params=_compiler_params(("arbitrary", "arbitrary")),
        name="attn_a",
    )(qaT, ka, vaT, bias_tiles, *lambdas, subln_w)


def _attn_b_kernel(qT_ref, k_ref, vT_ref, o_ref, *scratch, tq, tk):
    def finalize_fn(tile, acc_refs):
        rows = pl.ds(pl.multiple_of(tile * tq, tq), tq)
        o_ref[rows, :] = _normalized(acc_refs[0]).T.astype(BF16)

    _flash_loop(qT_ref, k_ref, vT_ref, None, finalize_fn, scratch, nmaps=1, tq=tq, tk=tk)


def _attn_b(qbT, kb, vbT, tq, tk):
    B, H, _, S = qbT.shape
    kernel = functools.partial(_attn_b_kernel, tq=tq, tk=tk)
    return pl.pallas_call(
        kernel,
        out_shape=jax.ShapeDtypeStruct((B * S, H * HEAD_W), BF16),
        grid=(B, H),
        in_specs=[
            pl.BlockSpec((1, 1, HEAD_W, S), lambda b, h: (b, h, 0, 0)),
            pl.BlockSpec((S, HEAD_W), lambda b, h: (b, h // B_GROUP)),
            pl.BlockSpec((1, 1, V_EXT_ROWS, S), lambda b, h: (b, h // B_GROUP, 0, 0)),
        ],
        out_specs=pl.BlockSpec((S, HEAD_W), lambda b, h: (b, h)),
        scratch_shapes=_flash_scratch(1, tq, tk, V_EXT_ROWS),
        compiler_params=_compiler_params(("arbitrary", "arbitrary")),
        name="attn_b",
    )(qbT, kb, vbT)


SCORE_BOUND = 60.0
NORM_MARGIN = 1.02
FAR_LEFT_BUCKET = REL_BUCKETS // 2 - 1
FAR_RIGHT_BUCKET = REL_BUCKETS - 1


def _bounded_scratch(nmaps, tq, tk):
    return [pltpu.VMEM((HEAD_W, tq), F32), pltpu.VMEM((tk, tq), BF16), pltpu.VMEM((2, 1, tq), F32)] * nmaps


def _bounded_stages(qT_ref, k_ref, vT_ref, scratch, *, nmaps, tq, tk):
    maps = [scratch[3 * mp:3 * mp + 3] for mp in range(nmaps)]

    def init():
        for acc_ref, p_ref, l_ref in maps:
            acc_ref[...] = jnp.zeros(acc_ref.shape, F32)
            p_ref[...] = jnp.zeros(p_ref.shape, BF16)
            l_ref[...] = jnp.zeros(l_ref.shape, F32)

    def accumulate(c, keep):
        v_t = vT_ref[0, 0, :HEAD_W, pl.ds(pl.multiple_of(c * tk, tk), tk)]
        for acc_ref, p_ref, _ in maps:
            acc_ref[...] = keep * acc_ref[...] + jnp.dot(v_t, p_ref[...], preferred_element_type=F32)

    def probs(tile, c, shift_fn):
        k_t = k_ref[pl.ds(pl.multiple_of(c * tk, tk), tk), :]
        keep = _keep_unless_first(c)
        for mp, (_, p_ref, l_ref) in enumerate(maps):
            q_t = qT_ref[0, mp, :, pl.ds(pl.multiple_of(tile * tq, tq), tq)]
            s = jnp.dot(k_t, q_t, preferred_element_type=F32)
            e = jnp.exp2(shift_fn(s))
            l_ref[tile % 2] = keep * l_ref[tile % 2] + jnp.sum(e, axis=0, keepdims=True)
            p_ref[...] = e.astype(BF16)

    def normalized(tile):
        return [acc_ref[...] * pl.reciprocal(l_ref[tile % 2]) for acc_ref, _, l_ref in maps]

    return init, accumulate, probs, normalized


def _keep_unless_first(c):
    return jnp.where(c == 0, 0.0, 1.0).astype(F32)


def _attn_a_bounded_kernel(rel_bias_ref, qT_ref, k_ref, vT_ref, bias_ref, lq1_ref, lk1_ref, lq2_ref, lk2_ref,
                           subln_ref, o_ref, *scratch, tq, tk, lambda_init):
    S = k_ref.shape[0]
    n_chunks, n_tiles, u = S // tk, S // tq, min(tq, tk)
    assert n_chunks >= 2
    h = pl.program_id(0)
    far_left = rel_bias_ref[FAR_LEFT_BUCKET, h] * LOG2_E
    far_right = rel_bias_ref[FAR_RIGHT_BUCKET, h] * LOG2_E
    init, accumulate, probs, normalized = _bounded_stages(qT_ref, k_ref, vT_ref, scratch, nmaps=2, tq=tq, tk=tk)

    def finalize(tile):
        o1T, o2T = normalized(tile)
        _diff_finalize(o1T, o2T, (lq1_ref, lk1_ref, lq2_ref, lk2_ref), subln_ref, o_ref, tile, tq, lambda_init)

    def step(tile, c, shift_fn, may_start_tile):
        c_prev = (c + n_chunks - 1) % n_chunks
        accumulate(c_prev, _keep_unless_first(c_prev))
        probs(tile, c, shift_fn)
        if may_start_tile:
            @pl.when((c == 0) & (tile > 0))
            def _():
                finalize(tile - 1)

    def tile_body(tile, carry):
        c_lo = jnp.maximum((tile * tq + tk - 1) // tk - 1, 0)
        c_hi = jnp.minimum((tile * tq + tq) // tk, n_chunks - 1)

        def far(const):
            return lambda s: s + const

        def near(c):
            idx = (c * tk - tile * tq + tk) // u + 1
            return lambda s: s + bias_ref[0, idx]

        def run(lo, hi, shift_of, may_start_tile):
            def body(c, carry):
                step(tile, c, shift_of(c), may_start_tile)
                return carry
            lax.fori_loop(lo, hi, body, 0)

        run(0, c_lo, lambda c: far(far_left), True)
        run(c_lo, c_hi + 1, near, True)
        run(c_hi + 1, n_chunks, lambda c: far(far_right), False)
        return carry

    init()
    lax.fori_loop(0, n_tiles, tile_body, 0)
    accumulate(n_chunks - 1, jnp.float32(1.0))
    finalize(n_tiles - 1)


def _attn_a_bounded(qaT, ka, vaT, bias_tiles, rel_bias, lambdas, subln_w, lambda_init, tq, tk):
    B, H, _, S = vaT.shape
    nt = bias_tiles.shape[1]
    vec = pl.BlockSpec((1, A_QK_DIM), lambda h, b: (0, 0))
    kernel = functools.partial(_attn_a_bounded_kernel, tq=tq, tk=tk, lambda_init=lambda_init)
    return pl.pallas_call(
        kernel,
        out_shape=jax.ShapeDtypeStruct((B * S, H * HEAD_W), BF16),
        grid=(H, B),
        in_specs=[
            pl.BlockSpec(memory_space=pltpu.SMEM),
            pl.BlockSpec((1, 2, HEAD_W, S), lambda h, b: (b, h, 0, 0)),
            pl.BlockSpec((S, HEAD_W), lambda h, b: (b, h)),
            pl.BlockSpec((1, 1, V_EXT_ROWS, S), lambda h, b: (b, h, 0, 0)),
            pl.BlockSpec((1, nt, tk, tq), lambda h, b: (h, 0, 0, 0)),
            vec, vec, vec, vec,
            pl.BlockSpec((1, A_V_DIM), lambda h, b: (0, 0)),
        ],
        out_specs=pl.BlockSpec((S, HEAD_W), lambda h, b: (b, h)),
        scratch_shapes=_bounded_scratch(2, tq, tk),
        compiler_params=_compiler_params(("arbitrary", "arbitrary")),
        name="attn_a_bounded",
    )(rel_bias.astype(F32), qaT, ka, vaT, bias_tiles, *lambdas, subln_w)


def _attn_b_bounded_kernel(qT_ref, k_ref, vT_ref, o_ref, *scratch, tq, tk):
    S = k_ref.shape[0]
    n_chunks, n_tiles = S // tk, S // tq
    assert n_chunks >= 2
    init, accumulate, probs, normalized = _bounded_stages(qT_ref, k_ref, vT_ref, scratch, nmaps=1, tq=tq, tk=tk)

    def finalize(tile):
        rows = pl.ds(pl.multiple_of(tile * tq, tq), tq)
        o_ref[rows, :] = normalized(tile)[0].T.astype(BF16)

    def body(g, carry):
        tile, c = g // n_chunks, g % n_chunks
        c_prev = (c + n_chunks - 1) % n_chunks
        accumulate(c_prev, _keep_unless_first(c_prev))
        probs(tile, c, lambda s: s)

        @pl.when((c == 0) & (tile > 0))
        def _():
            finalize(tile - 1)

        return carry

    init()
    lax.fori_loop(0, n_tiles * n_chunks, body, 0)
    accumulate(n_chunks - 1, jnp.float32(1.0))
    finalize(n_tiles - 1)


def _attn_b_bounded(qbT, kb, vbT, tq, tk):
    B, H, _, S = qbT.shape
    kernel = functools.partial(_attn_b_bounded_kernel, tq=tq, tk=tk)
    return pl.pallas_call(
        kernel,
        out_shape=jax.ShapeDtypeStruct((B * S, H * HEAD_W), BF16),
        grid=(B, H),
        in_specs=[
            pl.BlockSpec((1, 1, HEAD_W, S), lambda b, h: (b, h, 0, 0)),
            pl.BlockSpec((S, HEAD_W), lambda b, h: (b, h // B_GROUP)),
            pl.BlockSpec((1, 1, V_EXT_ROWS, S), lambda b, h: (b, h // B_GROUP, 0, 0)),
        ],
        out_specs=pl.BlockSpec((S, HEAD_W), lambda b, h: (b, h)),
        scratch_shapes=_bounded_scratch(1, tq, tk),
        compiler_params=_compiler_params(("arbitrary", "arbitrary")),
        name="attn_b_bounded",
    )(qbT, kb, vbT)


def _out_proj_kernel(x_ref, oa_ref, za_ref, ob_ref, zb_ref, ga_ref, gb_ref, wpa_ref, wpb_ref, wo_ref,
                     gf_ref, y_ref):
    ya = jnp.dot(oa_ref[...] * za_ref[...], wpa_ref[...], preferred_element_type=F32)
    yb = jnp.dot(ob_ref[...] * zb_ref[...], wpb_ref[...], preferred_element_type=F32)
    merged = ga_ref[...].astype(F32) * ya + gb_ref[...].astype(F32) * yb
    h = x_ref[...] + jnp.dot(merged.astype(BF16), wo_ref[...], preferred_element_type=F32)
    ms = jnp.mean(h * h, axis=-1, keepdims=True)
    y_ref[...] = h * lax.rsqrt(ms + EPS) * gf_ref[...]


def _out_proj(x2, oa, za, ob, zb, ga, gb, w_proj_a, w_proj_b, w_out, g_final, tm):
    T, D = x2.shape
    tok = pl.BlockSpec((tm, D), lambda t: (t, 0))

    def const_spec(shape):
        return pl.BlockSpec(shape, lambda t: (0,) * len(shape), pipeline_mode=pl.Buffered(1))

    return pl.pallas_call(
        _out_proj_kernel,
        out_shape=jax.ShapeDtypeStruct((T, D), F32),
        grid=(T // tm,),
        in_specs=[tok] * 7 + [const_spec((D, D))] * 3 + [const_spec((1, D))],
        out_specs=tok,
        compiler_params=_compiler_params(("arbitrary",)),
        name="out_proj",
    )(x2, oa, za, ob, zb, ga, gb, w_proj_a, w_proj_b, w_out, g_final.reshape(1, D))


IN_PROJ_ROWS = 256
OUT_PROJ_ROWS = 512
ATTN_A_TQ = 512
ATTN_A_TK = 1024
ATTN_B_TQ = 512
ATTN_B_TK = 2048
ATTN_B_BOUNDED_TQ = 1024
ATTN_B_BOUNDED_TK = 2048


def _layer(x, l, g_norm, w_in, lambdas, subln_w, q_norm_b, k_norm_b, w_proj_a, w_proj_b, w_out,
           bias_tiles, rel_bias, g_final):
    B, S, D = x.shape
    offs = [0]
    for n in IN_SIZES:
        offs.append(offs[-1] + n)
    w_bf = w_in.astype(BF16)
    w_sections = [w_bf[:, offs[k]:offs[k + 1]] for k in range(len(IN_SIZES))]
    qaT, ka, vaT, za, qbT, kb, vbT, zb, ga, gb, qa_sq, ka_sq = _in_proj(x, g_norm, w_sections, q_norm_b,
                                                                        k_norm_b, IN_PROJ_ROWS)
    lambda_init = 0.8 - 0.6 * math.exp(-0.3 * l)
    lambdas = [v.reshape(1, A_QK_DIM).astype(F32) for v in lambdas]
    subln = subln_w.reshape(1, A_V_DIM).astype(F32)

    bound_a = (jnp.sqrt(jnp.max(qa_sq) * jnp.max(ka_sq)) * NORM_MARGIN
               + jnp.max(jnp.abs(rel_bias)) * LOG2_E)
    oa = lax.cond(
        bound_a <= SCORE_BOUND,
        lambda: _attn_a_bounded(qaT, ka, vaT, bias_tiles, rel_bias, lambdas, subln, lambda_init,
                                ATTN_A_TQ, ATTN_A_TK),
        lambda: _attn_a(qaT, ka, vaT, bias_tiles, lambdas, subln, lambda_init, ATTN_A_TQ, ATTN_A_TK))
    bound_b = (B_HEAD_DIM * jnp.max(jnp.abs(q_norm_b)) * jnp.max(jnp.abs(k_norm_b))
               * (B_HEAD_DIM ** -0.5 * LOG2_E) * NORM_MARGIN)
    ob = lax.cond(
        bound_b <= SCORE_BOUND,
        lambda: _attn_b_bounded(qbT, kb, vbT, ATTN_B_BOUNDED_TQ, ATTN_B_BOUNDED_TK),
        lambda: _attn_b(qbT, kb, vbT, ATTN_B_TQ, ATTN_B_TK))
    y = _out_proj(x.reshape(B * S, D), oa, za, ob, zb, ga, gb, w_proj_a.astype(BF16), w_proj_b.astype(BF16),
                  w_out.astype(BF16), g_final, OUT_PROJ_ROWS)
    return y.reshape(B, S, D)


def kernel(x_prompt, x_sample, g_norm, w_in, lambda_q1, lambda_k1, lambda_q2, lambda_k2, subln_w,
           q_norm_b, k_norm_b, w_proj_a, w_proj_b, w_out, rel_bias, g_final):
    assert DEPTH == 1 and g_norm.shape[0] == DEPTH
    bias_tiles = _bias_tiles(rel_bias, ATTN_A_TQ, ATTN_A_TK)
    l = 0
    outs = []
    for x in (x_prompt, x_sample):
        outs.append(_layer(x, l, g_norm[l], w_in[l],
                           (lambda_q1[l], lambda_k1[l], lambda_q2[l], lambda_k2[l]), subln_w[l],
                           q_norm_b[l], k_norm_b[l], w_proj_a[l], w_proj_b[l], w_out[l], bias_tiles, rel_bias,
                           g_final))
    return tuple(outs)
```

```python
import functools
import math

import jax
import jax.numpy as jnp
from jax import lax
from jax.experimental import pallas as pl
from jax.experimental.pallas import tpu as pltpu

D_MODEL = 1024
A_HEADS = 8
A_QK_DIM = 64
A_V_DIM = 128
B_HEADS = 8
B_KV_HEADS = 2
B_GROUP = B_HEADS // B_KV_HEADS
B_HEAD_DIM = 128
HEAD_W = 128
V_ONES_ROWS = 16
V_EXT_ROWS = HEAD_W + V_ONES_ROWS
ROPE_AXIS_DIM = 64
ROPE_THETA = 10000.0
GRID_W = 64
REL_BUCKETS = 32
REL_MAX_DIST = 128
EPS = 1e-6
DEPTH = 1
IN_SIZES = (1024, 1024, 1024, 1024, 1024, 256, 256, 1024, 1024, 1024)

BIAS_SATURATION_DIST = 128

VMEM_LIMIT_BYTES = 56 * 1024 * 1024

LOG2_E = math.log2(math.e)

F32 = jnp.float32
BF16 = jnp.bfloat16


def _compiler_params(semantics):
    return pltpu.CompilerParams(dimension_semantics=semantics, vmem_limit_bytes=VMEM_LIMIT_BYTES)


def _rel_bucket(rel):
    half = REL_BUCKETS // 2
    max_exact = half // 2
    ret = (rel > 0).astype(jnp.int32) * half
    n = jnp.abs(rel)
    nf = jnp.maximum(n, 1).astype(F32)
    large = max_exact + (jnp.log(nf / max_exact) / math.log(REL_MAX_DIST / max_exact)
                         * (half - max_exact)).astype(jnp.int32)
    large = jnp.minimum(large, half - 1)
    return ret + jnp.where(n < max_exact, n, large)


def _num_bias_tiles(tq, tk):
    u = min(tq, tk)
    return tk // u + tq // u + 3


def _bias_bucket_tiles(tq, tk):
    u = min(tq, tk)
    assert tq % u == 0 and tk % u == 0 and u >= BIAS_SATURATION_DIST
    nt = _num_bias_tiles(tq, tk)
    o = (jnp.arange(nt, dtype=jnp.int32) - (tk // u + 1)) * u
    kk = jnp.arange(tk, dtype=jnp.int32)
    qq = jnp.arange(tq, dtype=jnp.int32)
    rel = o[:, None, None] + kk[None, :, None] - qq[None, None, :]
    return _rel_bucket(rel)


def _tile_bucket_range(t, tq, tk):
    u = min(tq, tk)
    o = (t - (tk // u + 1)) * u
    half = REL_BUCKETS // 2
    if t == 0:
        return half - 1, half
    if t == _num_bias_tiles(tq, tk) - 1:
        return REL_BUCKETS - 1, REL_BUCKETS
    lo = 0 if o - tq + 1 <= 0 else half
    hi = REL_BUCKETS if o + tk - 1 > 0 else half
    return lo, hi


def _bias_tiles_kernel(bucket_ref, rel_bias_ref, out_ref, *, tq, tk):
    t, h = pl.program_id(0), pl.program_id(1)
    for ts in range(_num_bias_tiles(tq, tk)):
        lo, hi = _tile_bucket_range(ts, tq, tk)

        @pl.when(t == ts)
        def _():
            bucket = bucket_ref[0]
            acc = jnp.full(bucket.shape, rel_bias_ref[lo, h], F32)
            for b in range(lo + 1, hi):
                acc = jnp.where(bucket == b, rel_bias_ref[b, h], acc)
            out_ref[0, 0] = (acc * LOG2_E).astype(out_ref.dtype)


def _bias_tiles(rel_bias, tq, tk, dtype):
    buckets = _bias_bucket_tiles(tq, tk)
    nt = buckets.shape[0]
    return pl.pallas_call(
        functools.partial(_bias_tiles_kernel, tq=tq, tk=tk),
        out_shape=jax.ShapeDtypeStruct((A_HEADS, nt, tk, tq), dtype),
        grid=(nt, A_HEADS),
        in_specs=[
            pl.BlockSpec((1, tk, tq), lambda t, h: (t, 0, 0)),
            pl.BlockSpec(memory_space=pltpu.SMEM),
        ],
        out_specs=pl.BlockSpec((1, 1, tk, tq), lambda t, h: (h, t, 0, 0)),
        compiler_params=_compiler_params(("arbitrary", "arbitrary")),
        name="bias_tiles",
    )(buckets, rel_bias.astype(F32))


def _axial_tables(S):
    rows = S // GRID_W
    row_idx = jnp.repeat(jnp.arange(rows, dtype=F32), GRID_W)
    col_idx = jnp.tile(jnp.arange(GRID_W, dtype=F32), rows)
    inv_freq = ROPE_THETA ** (-jnp.arange(0, ROPE_AXIS_DIM, 2, dtype=F32) / ROPE_AXIS_DIM)
    ang_r = row_idx[:, None] * inv_freq[None, :]
    ang_c = col_idx[:, None] * inv_freq[None, :]
    cr, sr, cc, sc = jnp.cos(ang_r), jnp.sin(ang_r), jnp.cos(ang_c), jnp.sin(ang_c)
    cos_t = jnp.concatenate([cr, cr, cc, cc], axis=-1)
    sin_t = jnp.concatenate([-sr, sr, -sc, sc], axis=-1)
    return cos_t, sin_t


def _head_rms(t, gain):
    ms = jnp.mean(t * t, axis=-1, keepdims=True)
    return t * lax.rsqrt(ms + EPS) * gain


def _rope(t, cos_t, sin_t):
    lane = lax.broadcasted_iota(jnp.int32, t.shape, 1)
    quarter = ROPE_AXIS_DIM // 2
    partner = jnp.where((lane & quarter) == 0,
                        pltpu.roll(t, HEAD_W - quarter, 1), pltpu.roll(t, quarter, 1))
    return t * cos_t + partner * sin_t


def _in_proj_kernel(x_ref, g_ref, cos_ref, sin_ref, qn_ref, kn_ref, head_ones_ref,
                    w_qa, w_ka, w_va, w_za, w_qb, w_kb, w_vb, w_zb, w_ga, w_gb,
                    qaT_ref, ka_ref, vaT_ref, za_ref, qbT_ref, kb_ref, vbT_ref, zb_ref, ga_ref, gb_ref,
                    qa_sq_ref, ka_sq_ref):
    x = x_ref[...]
    ms = jnp.mean(x * x, axis=-1, keepdims=True)
    xn = (x * lax.rsqrt(ms + EPS) * g_ref[...]).astype(BF16)

    def proj(w_ref):
        return jnp.dot(xn, w_ref[...], preferred_element_type=F32)

    def head(p, h):
        return p[:, h * HEAD_W:(h + 1) * HEAD_W]

    def max_head_sq_norm(p):
        sq = jnp.dot((p * p).astype(BF16), head_ones_ref[...], preferred_element_type=F32)
        return jnp.max(sq.reshape(sq.shape[0] // 8, 8, HEAD_W), axis=0)

    def store_vT(ref, h, t):
        ref[0, h, :HEAD_W] = t.T.astype(BF16)
        ref[0, h, HEAD_W:] = jnp.ones((V_ONES_ROWS, t.shape[0]), BF16)

    p = proj(w_qa) * (A_QK_DIM ** -0.5 * LOG2_E)
    qa_sq_ref[0] = max_head_sq_norm(p)
    first_map = lax.broadcasted_iota(jnp.int32, (HEAD_W, x.shape[0]), 0) < A_QK_DIM
    for h in range(A_HEADS):
        t = head(p, h).T
        qaT_ref[0, 2 * h] = jnp.where(first_map, t, 0.0).astype(BF16)
        qaT_ref[0, 2 * h + 1] = jnp.where(first_map, 0.0, t).astype(BF16)
    p = proj(w_ka)
    ka_sq_ref[0] = max_head_sq_norm(p)
    ka_ref[...] = p.astype(BF16)
    p = proj(w_va)
    for h in range(A_HEADS):
        store_vT(vaT_ref, h, head(p, h))
    p = proj(w_za)
    za_ref[...] = (p * jax.nn.sigmoid(p)).astype(BF16)

    cos_t = cos_ref[...]
    sin_t = sin_ref[...]
    p = proj(w_qb)
    for h in range(B_HEADS):
        t = _rope(_head_rms(head(p, h), qn_ref[...]), cos_t, sin_t) * (B_HEAD_DIM ** -0.5 * LOG2_E)
        qbT_ref[0, h] = t.T.astype(BF16)
    p = proj(w_kb)
    for h in range(B_KV_HEADS):
        t = _rope(_head_rms(head(p, h), kn_ref[...]), cos_t, sin_t)
        kb_ref[:, h * HEAD_W:(h + 1) * HEAD_W] = t.astype(BF16)
    p = proj(w_vb)
    for h in range(B_KV_HEADS):
        store_vT(vbT_ref, h, head(p, h))
    p = proj(w_zb)
    zb_ref[...] = (p * jax.nn.sigmoid(p)).astype(BF16)
    ga_ref[...] = jax.nn.sigmoid(proj(w_ga)).astype(BF16)
    gb_ref[...] = jax.nn.sigmoid(proj(w_gb)).astype(BF16)


def _in_proj(x, g_norm, w_sections, q_norm_b, k_norm_b, tm):
    B, S, D = x.shape
    T = B * S
    assert S % tm == 0
    tiles_per_seq = S // tm
    cos_t, sin_t = _axial_tables(S)
    x2 = x.reshape(T, D)

    def tok_spec(width):
        return pl.BlockSpec((tm, width), lambda t: (t, 0))

    def headT_spec(nh, rows=HEAD_W):
        return pl.BlockSpec((1, nh, rows, tm), lambda t: (t // tiles_per_seq, 0, 0, t % tiles_per_seq))

    def const_spec(shape):
        return pl.BlockSpec(shape, lambda t: (0,) * len(shape), pipeline_mode=pl.Buffered(1))

    rope_spec = pl.BlockSpec((tm, HEAD_W), lambda t: (t % tiles_per_seq, 0))
    in_specs = [tok_spec(D), const_spec((1, D)), rope_spec, rope_spec,
                const_spec((1, HEAD_W)), const_spec((1, HEAD_W)), const_spec((A_HEADS * HEAD_W, HEAD_W))]
    in_specs += [const_spec(w.shape) for w in w_sections]
    head_ones = (jnp.arange(A_HEADS * HEAD_W)[:, None] // HEAD_W == jnp.arange(HEAD_W)[None, :]).astype(BF16)
    sq_out = jax.ShapeDtypeStruct((T // tm, 8, HEAD_W), F32)
    sq_spec = pl.BlockSpec((1, 8, HEAD_W), lambda t: (t, 0, 0))

    def tok_out(width):
        return jax.ShapeDtypeStruct((T, width), BF16)

    def headT_out(nh, rows=HEAD_W):
        return jax.ShapeDtypeStruct((B, nh, rows, S), BF16)

    out_shape = [headT_out(2 * A_HEADS), tok_out(1024), headT_out(A_HEADS, V_EXT_ROWS), tok_out(1024),
                 headT_out(B_HEADS), tok_out(256), headT_out(B_KV_HEADS, V_EXT_ROWS), tok_out(1024),
                 tok_out(1024), tok_out(1024), sq_out, sq_out]
    out_specs = [headT_spec(2 * A_HEADS), tok_spec(1024), headT_spec(A_HEADS, V_EXT_ROWS), tok_spec(1024),
                 headT_spec(B_HEADS), tok_spec(256), headT_spec(B_KV_HEADS, V_EXT_ROWS), tok_spec(1024),
                 tok_spec(1024), tok_spec(1024), sq_spec, sq_spec]
    return pl.pallas_call(
        _in_proj_kernel,
        out_shape=out_shape,
        grid=(T // tm,),
        in_specs=in_specs,
        out_specs=out_specs,
        compiler_params=_compiler_params(("arbitrary",)),
        name="in_proj",
    )(x2, g_norm.reshape(1, D), cos_t, sin_t, q_norm_b.reshape(1, HEAD_W), k_norm_b.reshape(1, HEAD_W),
      head_ones, *w_sections)


def _flash_scratch(nmaps, tq, tk, dv):
    per_map = [pltpu.VMEM((1, tq), F32), pltpu.VMEM((dv, tq), F32),
               pltpu.VMEM((tk, tq), F32), pltpu.VMEM((tk, tq), BF16), pltpu.VMEM((1, tq), F32),
               pltpu.VMEM((1, tq), F32)]
    return per_map * nmaps


def _flash_loop(qT_ref, k_ref, vT_ref, bias_fn, finalize_fn, scratch, *, nmaps, tq, tk):
    maps = [scratch[6 * mp:6 * mp + 6] for mp in range(nmaps)]
    S = k_ref.shape[0]
    n_chunks = S // tk
    n_items = (S // tq) * n_chunks
    assert n_chunks >= 2

    for _, acc_ref, _, _, _, _ in maps:
        acc_ref[...] = jnp.zeros(acc_ref.shape, F32)

    def scores(f):
        tile, c = f // n_chunks, f % n_chunks
        k_t = k_ref[pl.ds(pl.multiple_of(c * tk, tk), tk), :]
        bias = None if bias_fn is None else bias_fn(tile, c)
        for mp in range(nmaps):
            q_t = qT_ref[0, mp, :, pl.ds(pl.multiple_of(tile * tq, tq), tq)]
            s = jnp.dot(k_t, q_t, preferred_element_type=F32)
            if bias is not None:
                s = s + bias
            _, _, s_ref, _, _, smax_ref = maps[mp]
            s_ref[...] = s
            smax_ref[...] = jnp.max(s, axis=0, keepdims=True)

    def probs(f):
        first_chunk = (f % n_chunks) == 0
        for m_ref, _, s_ref, p_ref, alpha_ref, smax_ref in maps:
            m_prev = jnp.where(first_chunk, -jnp.inf, m_ref[...])
            m_new = jnp.maximum(m_prev, smax_ref[...])
            m_ref[...] = m_new
            alpha_ref[...] = jnp.exp2(m_prev - m_new)
            p_ref[...] = jnp.exp2(s_ref[...] - m_new).astype(BF16)

    def accumulate(f):
        c = f % n_chunks
        v_t = vT_ref[0, 0, :, pl.ds(pl.multiple_of(c * tk, tk), tk)]
        for _, acc_ref, _, p_ref, alpha_ref, _ in maps:
            acc_ref[...] = alpha_ref[...] * acc_ref[...] + jnp.dot(v_t, p_ref[...], preferred_element_type=F32)

    def finalize(tile):
        finalize_fn(tile, [acc_ref for _, acc_ref, _, _, _, _ in maps])

    scores(0)
    probs(0)
    scores(1)

    def body(f, carry):
        accumulate(f - 1)
        probs(f)
        scores(f + 1)

        @pl.when((f - 1) % n_chunks == n_chunks - 1)
        def _():
            finalize((f - 1) // n_chunks)

        return carry

    lax.fori_loop(1, n_items - 1, body, 0)
    accumulate(n_items - 2)
    probs(n_items - 1)
    accumulate(n_items - 1)
    finalize(S // tq - 1)


def _normalized(acc_ref):
    return acc_ref[:HEAD_W, :] * pl.reciprocal(acc_ref[HEAD_W:HEAD_W + 1, :])


def _diff_finalize(o1T, o2T, lambda_refs, subln_ref, o_ref, tile, tq, lambda_init):
    lq1_ref, lk1_ref, lq2_ref, lk2_ref = lambda_refs
    lam = (jnp.exp(jnp.sum(lq1_ref[...] * lk1_ref[...], axis=-1, keepdims=True))
           - jnp.exp(jnp.sum(lq2_ref[...] * lk2_ref[...], axis=-1, keepdims=True))
           + lambda_init)
    oT = o1T - lam * o2T
    ms = jnp.mean(oT * oT, axis=0, keepdims=True)
    oT = oT * lax.rsqrt(ms + EPS)
    rows = pl.ds(pl.multiple_of(tile * tq, tq), tq)
    o_ref[rows, :] = (oT.T * (subln_ref[...] * (1.0 - lambda_init))).astype(BF16)


def _attn_a_kernel(qT_ref, k_ref, vT_ref, bias_ref, lq1_ref, lk1_ref, lq2_ref, lk2_ref, subln_ref,
                   o_ref, *scratch, tq, tk, lambda_init):
    u = min(tq, tk)

    def bias_fn(tile, c):
        offset = (c * tk - tile * tq) // u
        idx = jnp.clip(offset, -(tk // u) - 1, tq // u + 1) + (tk // u + 1)
        return bias_ref[0, idx]

    def finalize_fn(tile, acc_refs):
        _diff_finalize(_normalized(acc_refs[0]), _normalized(acc_refs[1]),
                       (lq1_ref, lk1_ref, lq2_ref, lk2_ref), subln_ref, o_ref, tile, tq, lambda_init)

    _flash_loop(qT_ref, k_ref, vT_ref, bias_fn, finalize_fn, scratch, nmaps=2, tq=tq, tk=tk)


def _attn_a(qaT, ka, vaT, bias_tiles, lambdas, subln_w, lambda_init, tq, tk):
    B, H, _, S = vaT.shape
    nt = bias_tiles.shape[1]
    vec = pl.BlockSpec((1, A_QK_DIM), lambda b, h: (0, 0))
    kernel = functools.partial(_attn_a_kernel, tq=tq, tk=tk, lambda_init=lambda_init)
    return pl.pallas_call(
        kernel,
        out_shape=jax.ShapeDtypeStruct((B * S, H * HEAD_W), BF16),
        grid=(B, H),
        in_specs=[
            pl.BlockSpec((1, 2, HEAD_W, S), lambda b, h: (b, h, 0, 0)),
            pl.BlockSpec((S, HEAD_W), lambda b, h: (b, h)),
            pl.BlockSpec((1, 1, V_EXT_ROWS, S), lambda b, h: (b, h, 0, 0)),
            pl.BlockSpec((1, nt, tk, tq), lambda b, h: (h, 0, 0, 0), pipeline_mode=pl.Buffered(1)),
            vec, vec, vec, vec,
            pl.BlockSpec((1, A_V_DIM), lambda b, h: (0, 0)),
        ],
        out_specs=pl.BlockSpec((S, HEAD_W), lambda b, h: (b, h)),
        scratch_shapes=_flash_scratch(2, tq, tk, V_EXT_ROWS),
        compiler_params=_compiler_params(("arbitrary", "arbitrary")),
        name="attn_a",
    )(qaT, ka, vaT, bias_tiles, *lambdas, subln_w)


def _attn_b_kernel(qT_ref, k_ref, vT_ref, o_ref, *scratch, tq, tk):
    def finalize_fn(tile, acc_refs):
        rows = pl.ds(pl.multiple_of(tile * tq, tq), tq)
        o_ref[rows, :] = _normalized(acc_refs[0]).T.astype(BF16)

    _flash_loop(qT_ref, k_ref, vT_ref, None, finalize_fn, scratch, nmaps=1, tq=tq, tk=tk)


def _attn_b(qbT, kb, vbT, tq, tk):
    B, H, _, S = qbT.shape
    kernel = functools.partial(_attn_b_kernel, tq=tq, tk=tk)
    return pl.pallas_call(
        kernel,
        out_shape=jax.ShapeDtypeStruct((B * S, H * HEAD_W), BF16),
        grid=(B, H),
        in_specs=[
            pl.BlockSpec((1, 1, HEAD_W, S), lambda b, h: (b, h, 0, 0)),
            pl.BlockSpec((S, HEAD_W), lambda b, h: (b, h // B_GROUP)),
            pl.BlockSpec((1, 1, V_EXT_ROWS, S), lambda b, h: (b, h // B_GROUP, 0, 0)),
        ],
        out_specs=pl.BlockSpec((S, HEAD_W), lambda b, h: (b, h)),
        scratch_shapes=_flash_scratch(1, tq, tk, V_EXT_ROWS),
        compiler_params=_compiler_params(("arbitrary", "arbitrary")),
        name="attn_b",
    )(qbT, kb, vbT)


SCORE_BOUND = 60.0
NORM_MARGIN = 1.02


def _bounded_scratch(nmaps, tq, tk):
    return [pltpu.VMEM((HEAD_W, tq), F32), pltpu.VMEM((tk, tq), BF16), pltpu.VMEM((2, 1, tq), F32)] * nmaps


def _bounded_stages(qT_ref, k_ref, vT_ref, scratch, *, nmaps, tq, tk):
    maps = [scratch[3 * mp:3 * mp + 3] for mp in range(nmaps)]

    def init():
        for acc_ref, p_ref, l_ref in maps:
            acc_ref[...] = jnp.zeros(acc_ref.shape, F32)
            p_ref[...] = jnp.zeros(p_ref.shape, BF16)
            l_ref[...] = jnp.zeros(l_ref.shape, F32)

    def accumulate(c, keep):
        v_t = vT_ref[0, 0, :HEAD_W, pl.ds(pl.multiple_of(c * tk, tk), tk)]
        for acc_ref, p_ref, _ in maps:
            acc_ref[...] = keep * acc_ref[...] + jnp.dot(v_t, p_ref[...], preferred_element_type=F32)

    def probs(tile, c, shift_fn):
        k_t = k_ref[pl.ds(pl.multiple_of(c * tk, tk), tk), :]
        keep = _keep_unless_first(c)
        for mp, (_, p_ref, l_ref) in enumerate(maps):
            q_t = qT_ref[0, mp, :, pl.ds(pl.multiple_of(tile * tq, tq), tq)]
            s = jnp.dot(k_t, q_t, preferred_element_type=F32)
            e = jnp.exp2(shift_fn(s))
            l_ref[tile % 2] = keep * l_ref[tile % 2] + jnp.sum(e, axis=0, keepdims=True)
            p_ref[...] = e.astype(BF16)

    def normalized(tile):
        return [acc_ref[...] * pl.reciprocal(l_ref[tile % 2]) for acc_ref, _, l_ref in maps]

    return init, accumulate, probs, normalized


def _keep_unless_first(c):
    return jnp.where(c == 0, 0.0, 1.0).astype(F32)


def _bounded_loop(qT_ref, k_ref, vT_ref, scratch, shift_fn, finalize_fn, *, nmaps, tq, tk):
    S = k_ref.shape[0]
    n_chunks, n_tiles = S // tk, S // tq
    assert n_chunks >= 2
    init, accumulate, probs, normalized = _bounded_stages(qT_ref, k_ref, vT_ref, scratch,
                                                          nmaps=nmaps, tq=tq, tk=tk)

    def body(g, carry):
        tile, c = g // n_chunks, g % n_chunks
        c_prev = (c + n_chunks - 1) % n_chunks
        accumulate(c_prev, _keep_unless_first(c_prev))
        probs(tile, c, shift_fn(tile, c))

        @pl.when((c == 0) & (tile > 0))
        def _():
            finalize_fn(tile - 1, normalized(tile - 1))

        return carry

    init()
    lax.fori_loop(0, n_tiles * n_chunks, body, 0)
    accumulate(n_chunks - 1, jnp.float32(1.0))
    finalize_fn(n_tiles - 1, normalized(n_tiles - 1))


def _attn_a_bounded_kernel(qT_ref, k_ref, vT_ref, bias_ref, lq1_ref, lk1_ref, lq2_ref, lk2_ref,
                           subln_ref, o_ref, *scratch, tq, tk, lambda_init):
    u = min(tq, tk)

    def shift_fn(tile, c):
        offset = (c * tk - tile * tq) // u
        idx = jnp.clip(offset, -(tk // u) - 1, tq // u + 1) + (tk // u + 1)
        return lambda s: s + bias_ref[0, idx].astype(F32)

    def finalize_fn(tile, o_maps):
        _diff_finalize(o_maps[0], o_maps[1], (lq1_ref, lk1_ref, lq2_ref, lk2_ref), subln_ref, o_ref, tile, tq,
                       lambda_init)

    _bounded_loop(qT_ref, k_ref, vT_ref, scratch, shift_fn, finalize_fn, nmaps=2, tq=tq, tk=tk)


def _attn_a_bounded(qaT, ka, vaT, bias_tiles, lambdas, subln_w, lambda_init, tq, tk):
    B, H, _, S = vaT.shape
    nt = bias_tiles.shape[1]
    vec = pl.BlockSpec((1, A_QK_DIM), lambda h, b: (0, 0))
    kernel = functools.partial(_attn_a_bounded_kernel, tq=tq, tk=tk, lambda_init=lambda_init)
    return pl.pallas_call(
        kernel,
        out_shape=jax.ShapeDtypeStruct((B * S, H * HEAD_W), BF16),
        grid=(H, B),
        in_specs=[
            pl.BlockSpec((1, 2, HEAD_W, S), lambda h, b: (b, h, 0, 0)),
            pl.BlockSpec((S, HEAD_W), lambda h, b: (b, h)),
            pl.BlockSpec((1, 1, V_EXT_ROWS, S), lambda h, b: (b, h, 0, 0)),
            pl.BlockSpec((1, nt, tk, tq), lambda h, b: (h, 0, 0, 0), pipeline_mode=pl.Buffered(1)),
            vec, vec, vec, vec,
            pl.BlockSpec((1, A_V_DIM), lambda h, b: (0, 0)),
        ],
        out_specs=pl.BlockSpec((S, HEAD_W), lambda h, b: (b, h)),
        scratch_shapes=_bounded_scratch(2, tq, tk),
        compiler_params=_compiler_params(("arbitrary", "arbitrary")),
        name="attn_a_bounded",
    )(qaT, ka, vaT, bias_tiles, *lambdas, subln_w)


def _attn_b_bounded_kernel(qT_ref, k_ref, vT_ref, o_ref, *scratch, tq, tk):
    def finalize_fn(tile, o_maps):
        rows = pl.ds(pl.multiple_of(tile * tq, tq), tq)
        o_ref[rows, :] = o_maps[0].T.astype(BF16)

    _bounded_loop(qT_ref, k_ref, vT_ref, scratch, lambda tile, c: (lambda s: s), finalize_fn,
                  nmaps=1, tq=tq, tk=tk)


def _attn_b_bounded(qbT, kb, vbT, tq, tk):
    B, H, _, S = qbT.shape
    kernel = functools.partial(_attn_b_bounded_kernel, tq=tq, tk=tk)
    return pl.pallas_call(
        kernel,
        out_shape=jax.ShapeDtypeStruct((B * S, H * HEAD_W), BF16),
        grid=(B, H),
        in_specs=[
            pl.BlockSpec((1, 1, HEAD_W, S), lambda b, h: (b, h, 0, 0)),
            pl.BlockSpec((S, HEAD_W), lambda b, h: (b, h // B_GROUP)),
            pl.BlockSpec((1, 1, V_EXT_ROWS, S), lambda b, h: (b, h // B_GROUP, 0, 0)),
        ],
        out_specs=pl.BlockSpec((S, HEAD_W), lambda b, h: (b, h)),
        scratch_shapes=_bounded_scratch(1, tq, tk),
        compiler_params=_compiler_params(("arbitrary", "arbitrary")),
        name="attn_b_bounded",
    )(qbT, kb, vbT)


def _out_proj_kernel(x_ref, oa_ref, za_ref, ob_ref, zb_ref, ga_ref, gb_ref, wpa_ref, wpb_ref, wo_ref,
                     gf_ref, y_ref):
    ya = jnp.dot(oa_ref[...] * za_ref[...], wpa_ref[...], preferred_element_type=F32)
    yb = jnp.dot(ob_ref[...] * zb_ref[...], wpb_ref[...], preferred_element_type=F32)
    merged = ga_ref[...].astype(F32) * ya + gb_ref[...].astype(F32) * yb
    h = x_ref[...] + jnp.dot(merged.astype(BF16), wo_ref[...], preferred_element_type=F32)
    ms = jnp.mean(h * h, axis=-1, keepdims=True)
    y_ref[...] = h * lax.rsqrt(ms + EPS) * gf_ref[...]


def _out_proj(x2, oa, za, ob, zb, ga, gb, w_proj_a, w_proj_b, w_out, g_final, tm):
    T, D = x2.shape
    tok = pl.BlockSpec((tm, D), lambda t: (t, 0))

    def const_spec(shape):
        return pl.BlockSpec(shape, lambda t: (0,) * len(shape), pipeline_mode=pl.Buffered(1))

    return pl.pallas_call(
        _out_proj_kernel,
        out_shape=jax.ShapeDtypeStruct((T, D), F32),
        grid=(T // tm,),
        in_specs=[tok] * 7 + [const_spec((D, D))] * 3 + [const_spec((1, D))],
        out_specs=tok,
        compiler_params=_compiler_params(("arbitrary",)),
        name="out_proj",
    )(x2, oa, za, ob, zb, ga, gb, w_proj_a, w_proj_b, w_out, g_final.reshape(1, D))


IN_PROJ_ROWS = 256
OUT_PROJ_ROWS = 512
ATTN_A_TQ = 512
ATTN_A_TK = 1024
ATTN_B_TQ = 512
ATTN_B_TK = 2048
ATTN_A_BOUNDED_TQ = 1024
ATTN_A_BOUNDED_TK = 1024
ATTN_B_BOUNDED_TQ = 1024
ATTN_B_BOUNDED_TK = 2048


def _layer(x, l, g_norm, w_in, lambdas, subln_w, q_norm_b, k_norm_b, w_proj_a, w_proj_b, w_out,
           bounded_bias_tiles, rel_bias, g_final):
    B, S, D = x.shape
    offs = [0]
    for n in IN_SIZES:
        offs.append(offs[-1] + n)
    w_bf = w_in.astype(BF16)
    w_sections = [w_bf[:, offs[k]:offs[k + 1]] for k in range(len(IN_SIZES))]
    qaT, ka, vaT, za, qbT, kb, vbT, zb, ga, gb, qa_sq, ka_sq = _in_proj(x, g_norm, w_sections, q_norm_b,
                                                                        k_norm_b, IN_PROJ_ROWS)
    lambda_init = 0.8 - 0.6 * math.exp(-0.3 * l)
    lambdas = [v.reshape(1, A_QK_DIM).astype(F32) for v in lambdas]
    subln = subln_w.reshape(1, A_V_DIM).astype(F32)

    bound_a = (jnp.sqrt(jnp.max(qa_sq) * jnp.max(ka_sq)) * NORM_MARGIN
               + jnp.max(jnp.abs(rel_bias)) * LOG2_E)
    oa = lax.cond(
        bound_a <= SCORE_BOUND,
        lambda: _attn_a_bounded(qaT, ka, vaT, bounded_bias_tiles, lambdas, subln, lambda_init,
                                ATTN_A_BOUNDED_TQ, ATTN_A_BOUNDED_TK),
        lambda: _attn_a(qaT, ka, vaT, _bias_tiles(rel_bias, ATTN_A_TQ, ATTN_A_TK, F32), lambdas, subln,
                        lambda_init, ATTN_A_TQ, ATTN_A_TK))
    bound_b = (B_HEAD_DIM * jnp.max(jnp.abs(q_norm_b)) * jnp.max(jnp.abs(k_norm_b))
               * (B_HEAD_DIM ** -0.5 * LOG2_E) * NORM_MARGIN)
    ob = lax.cond(
        bound_b <= SCORE_BOUND,
        lambda: _attn_b_bounded(qbT, kb, vbT, ATTN_B_BOUNDED_TQ, ATTN_B_BOUNDED_TK),
        lambda: _attn_b(qbT, kb, vbT, ATTN_B_TQ, ATTN_B_TK))
    y = _out_proj(x.reshape(B * S, D), oa, za, ob, zb, ga, gb, w_proj_a.astype(BF16), w_proj_b.astype(BF16),
                  w_out.astype(BF16), g_final, OUT_PROJ_ROWS)
    return y.reshape(B, S, D)


def kernel(x_prompt, x_sample, g_norm, w_in, lambda_q1, lambda_k1, lambda_q2, lambda_k2, subln_w,
           q_norm_b, k_norm_b, w_proj_a, w_proj_b, w_out, rel_bias, g_final):
    assert DEPTH == 1 and g_norm.shape[0] == DEPTH
    bounded_bias_tiles = _bias_tiles(rel_bias, ATTN_A_BOUNDED_TQ, ATTN_A_BOUNDED_TK, F32)
    l = 0
    outs = []
    for x in (x_prompt, x_sample):
        outs.append(_layer(x, l, g_norm[l], w_in[l],
                           (lambda_q1[l], lambda_k1[l], lambda_q2[l], lambda_k2[l]), subln_w[l],
                           q_norm_b[l], k_norm_b[l], w_proj_a[l], w_proj_b[l], w_out[l], bounded_bias_tiles,
                           rel_bias, g_final))
    return tuple(outs)
```

```python
import functools
import math

import jax
import jax.numpy as jnp
from jax import lax
from jax.experimental import pallas as pl
from jax.experimental.pallas import tpu as pltpu

D_MODEL = 1024
A_HEADS = 8
A_QK_DIM = 64
A_V_DIM = 128
B_HEADS = 8
B_KV_HEADS = 2
B_GROUP = B_HEADS // B_KV_HEADS
B_HEAD_DIM = 128
HEAD_W = 128
V_ONES_ROWS = 16
V_EXT_ROWS = HEAD_W + V_ONES_ROWS
ROPE_AXIS_DIM = 64
ROPE_THETA = 10000.0
GRID_W = 64
REL_BUCKETS = 32
REL_MAX_DIST = 128
EPS = 1e-6
DEPTH = 1
IN_SIZES = (1024, 1024, 1024, 1024, 1024, 256, 256, 1024, 1024, 1024)

BIAS_SATURATION_DIST = 128

VMEM_LIMIT_BYTES = 56 * 1024 * 1024

LOG2_E = math.log2(math.e)

F32 = jnp.float32
BF16 = jnp.bfloat16


def _compiler_params(semantics):
    return pltpu.CompilerParams(dimension_semantics=semantics, vmem_limit_bytes=VMEM_LIMIT_BYTES)


def _rel_bucket(rel):
    half = REL_BUCKETS // 2
    max_exact = half // 2
    ret = (rel > 0).astype(jnp.int32) * half
    n = jnp.abs(rel)
    nf = jnp.maximum(n, 1).astype(F32)
    large = max_exact + (jnp.log(nf / max_exact) / math.log(REL_MAX_DIST / max_exact)
                         * (half - max_exact)).astype(jnp.int32)
    large = jnp.minimum(large, half - 1)
    return ret + jnp.where(n < max_exact, n, large)


def _num_bias_tiles(tq, tk):
    u = min(tq, tk)
    return tk // u + tq // u + 3


def _bias_bucket_tiles(tq, tk):
    u = min(tq, tk)
    assert tq % u == 0 and tk % u == 0 and u >= BIAS_SATURATION_DIST
    nt = _num_bias_tiles(tq, tk)
    o = (jnp.arange(nt, dtype=jnp.int32) - (tk // u + 1)) * u
    kk = jnp.arange(tk, dtype=jnp.int32)
    qq = jnp.arange(tq, dtype=jnp.int32)
    rel = o[:, None, None] + kk[None, :, None] - qq[None, None, :]
    return _rel_bucket(rel)


def _tile_bucket_range(t, tq, tk):
    u = min(tq, tk)
    o = (t - (tk // u + 1)) * u
    half = REL_BUCKETS // 2
    if t == 0:
        return half - 1, half
    if t == _num_bias_tiles(tq, tk) - 1:
        return REL_BUCKETS - 1, REL_BUCKETS
    lo = 0 if o - tq + 1 <= 0 else half
    hi = REL_BUCKETS if o + tk - 1 > 0 else half
    return lo, hi


def _bias_tiles_kernel(bucket_ref, rel_bias_ref, out_ref, *, tq, tk):
    t, h = pl.program_id(0), pl.program_id(1)
    for ts in range(_num_bias_tiles(tq, tk)):
        lo, hi = _tile_bucket_range(ts, tq, tk)

        @pl.when(t == ts)
        def _():
            bucket = bucket_ref[0]
            acc = jnp.full(bucket.shape, rel_bias_ref[lo, h], F32)
            for b in range(lo + 1, hi):
                acc = jnp.where(bucket == b, rel_bias_ref[b, h], acc)
            out_ref[0, 0] = (acc * LOG2_E).astype(out_ref.dtype)


def _bias_tiles(rel_bias, tq, tk, dtype):
    buckets = _bias_bucket_tiles(tq, tk)
    nt = buckets.shape[0]
    return pl.pallas_call(
        functools.partial(_bias_tiles_kernel, tq=tq, tk=tk),
        out_shape=jax.ShapeDtypeStruct((A_HEADS, nt, tk, tq), dtype),
        grid=(nt, A_HEADS),
        in_specs=[
            pl.BlockSpec((1, tk, tq), lambda t, h: (t, 0, 0)),
            pl.BlockSpec(memory_space=pltpu.SMEM),
        ],
        out_specs=pl.BlockSpec((1, 1, tk, tq), lambda t, h: (h, t, 0, 0)),
        compiler_params=_compiler_params(("arbitrary", "arbitrary")),
        name="bias_tiles",
    )(buckets, rel_bias.astype(F32))


def _axial_tables(S):
    rows = S // GRID_W
    row_idx = jnp.repeat(jnp.arange(rows, dtype=F32), GRID_W)
    col_idx = jnp.tile(jnp.arange(GRID_W, dtype=F32), rows)
    inv_freq = ROPE_THETA ** (-jnp.arange(0, ROPE_AXIS_DIM, 2, dtype=F32) / ROPE_AXIS_DIM)
    ang_r = row_idx[:, None] * inv_freq[None, :]
    ang_c = col_idx[:, None] * inv_freq[None, :]
    cr, sr, cc, sc = jnp.cos(ang_r), jnp.sin(ang_r), jnp.cos(ang_c), jnp.sin(ang_c)
    cos_t = jnp.concatenate([cr, cr, cc, cc], axis=-1)
    sin_t = jnp.concatenate([-sr, sr, -sc, sc], axis=-1)
    return cos_t, sin_t


def _head_rms(t, gain):
    ms = jnp.mean(t * t, axis=-1, keepdims=True)
    return t * lax.rsqrt(ms + EPS) * gain


def _rope(t, cos_t, sin_t):
    lane = lax.broadcasted_iota(jnp.int32, t.shape, 1)
    quarter = ROPE_AXIS_DIM // 2
    partner = jnp.where((lane & quarter) == 0,
                        pltpu.roll(t, HEAD_W - quarter, 1), pltpu.roll(t, quarter, 1))
    return t * cos_t + partner * sin_t


def _in_proj_kernel(x_ref, g_ref, cos_ref, sin_ref, qn_ref, kn_ref, head_ones_ref,
                    w_qa, w_ka, w_va, w_za, w_qb, w_kb, w_vb, w_zb, w_ga, w_gb,
                    qaT_ref, ka_ref, vaT_ref, za_ref, qbT_ref, kb_ref, vbT_ref, zb_ref, ga_ref, gb_ref,
                    qa_sq_ref, ka_sq_ref):
    x = x_ref[...]
    ms = jnp.mean(x * x, axis=-1, keepdims=True)
    xn = (x * lax.rsqrt(ms + EPS) * g_ref[...]).astype(BF16)

    def proj(w_ref):
        return jnp.dot(xn, w_ref[...], preferred_element_type=F32)

    def head(p, h):
        return p[:, h * HEAD_W:(h + 1) * HEAD_W]

    def max_head_sq_norm(p):
        sq = jnp.dot((p * p).astype(BF16), head_ones_ref[...], preferred_element_type=F32)
        return jnp.max(sq.reshape(sq.shape[0] // 8, 8, HEAD_W), axis=0)

    def store_vT(ref, h, t):
        ref[0, h, :HEAD_W] = t.T.astype(BF16)
        ref[0, h, HEAD_W:] = jnp.ones((V_ONES_ROWS, t.shape[0]), BF16)

    p = proj(w_qa) * (A_QK_DIM ** -0.5 * LOG2_E)
    qa_sq_ref[0] = max_head_sq_norm(p)
    first_map = lax.broadcasted_iota(jnp.int32, (HEAD_W, x.shape[0]), 0) < A_QK_DIM
    for h in range(A_HEADS):
        t = head(p, h).T
        qaT_ref[0, 2 * h] = jnp.where(first_map, t, 0.0).astype(BF16)
        qaT_ref[0, 2 * h + 1] = jnp.where(first_map, 0.0, t).astype(BF16)
    p = proj(w_ka)
    ka_sq_ref[0] = max_head_sq_norm(p)
    ka_ref[...] = p.astype(BF16)
    p = proj(w_va)
    for h in range(A_HEADS):
        store_vT(vaT_ref, h, head(p, h))
    p = proj(w_za)
    za_ref[...] = (p * jax.nn.sigmoid(p)).astype(BF16)

    cos_t = cos_ref[...]
    sin_t = sin_ref[...]
    p = proj(w_qb)
    for h in range(B_HEADS):
        t = _rope(_head_rms(head(p, h), qn_ref[...]), cos_t, sin_t) * (B_HEAD_DIM ** -0.5 * LOG2_E)
        qbT_ref[0, h] = t.T.astype(BF16)
    p = proj(w_kb)
    for h in range(B_KV_HEADS):
        t = _rope(_head_rms(head(p, h), kn_ref[...]), cos_t, sin_t)
        kb_ref[:, h * HEAD_W:(h + 1) * HEAD_W] = t.astype(BF16)
    p = proj(w_vb)
    for h in range(B_KV_HEADS):
        store_vT(vbT_ref, h, head(p, h))
    p = proj(w_zb)
    zb_ref[...] = (p * jax.nn.sigmoid(p)).astype(BF16)
    ga_ref[...] = jax.nn.sigmoid(proj(w_ga)).astype(BF16)
    gb_ref[...] = jax.nn.sigmoid(proj(w_gb)).astype(BF16)


def _in_proj(x, g_norm, w_sections, q_norm_b, k_norm_b, tm):
    B, S, D = x.shape
    T = B * S
    assert S % tm == 0
    tiles_per_seq = S // tm
    cos_t, sin_t = _axial_tables(S)
    x2 = x.reshape(T, D)

    def tok_spec(width):
        return pl.BlockSpec((tm, width), lambda t: (t, 0))

    def headT_spec(nh, rows=HEAD_W):
        return pl.BlockSpec((1, nh, rows, tm), lambda t: (t // tiles_per_seq, 0, 0, t % tiles_per_seq))

    def const_spec(shape):
        return pl.BlockSpec(shape, lambda t: (0,) * len(shape), pipeline_mode=pl.Buffered(1))

    rope_spec = pl.BlockSpec((tm, HEAD_W), lambda t: (t % tiles_per_seq, 0))
    in_specs = [tok_spec(D), const_spec((1, D)), rope_spec, rope_spec,
                const_spec((1, HEAD_W)), const_spec((1, HEAD_W)), const_spec((A_HEADS * HEAD_W, HEAD_W))]
    in_specs += [const_spec(w.shape) for w in w_sections]
    head_ones = (jnp.arange(A_HEADS * HEAD_W)[:, None] // HEAD_W == jnp.arange(HEAD_W)[None, :]).astype(BF16)
    sq_out = jax.ShapeDtypeStruct((T // tm, 8, HEAD_W), F32)
    sq_spec = pl.BlockSpec((1, 8, HEAD_W), lambda t: (t, 0, 0))

    def tok_out(width):
        return jax.ShapeDtypeStruct((T, width), BF16)

    def headT_out(nh, rows=HEAD_W):
        return jax.ShapeDtypeStruct((B, nh, rows, S), BF16)

    out_shape = [headT_out(2 * A_HEADS), tok_out(1024), headT_out(A_HEADS, V_EXT_ROWS), tok_out(1024),
                 headT_out(B_HEADS), tok_out(256), headT_out(B_KV_HEADS, V_EXT_ROWS), tok_out(1024),
                 tok_out(1024), tok_out(1024), sq_out, sq_out]
    out_specs = [headT_spec(2 * A_HEADS), tok_spec(1024), headT_spec(A_HEADS, V_EXT_ROWS), tok_spec(1024),
                 headT_spec(B_HEADS), tok_spec(256), headT_spec(B_KV_HEADS, V_EXT_ROWS), tok_spec(1024),
                 tok_spec(1024), tok_spec(1024), sq_spec, sq_spec]
    return pl.pallas_call(
        _in_proj_kernel,
        out_shape=out_shape,
        grid=(T // tm,),
        in_specs=in_specs,
        out_specs=out_specs,
        compiler_params=_compiler_params(("arbitrary",)),
        name="in_proj",
    )(x2, g_norm.reshape(1, D), cos_t, sin_t, q_norm_b.reshape(1, HEAD_W), k_norm_b.reshape(1, HEAD_W),
      head_ones, *w_sections)


def _flash_scratch(nmaps, tq, tk, dv):
    per_map = [pltpu.VMEM((1, tq), F32), pltpu.VMEM((dv, tq), F32),
               pltpu.VMEM((tk, tq), F32), pltpu.VMEM((tk, tq), BF16), pltpu.VMEM((1, tq), F32),
               pltpu.VMEM((1, tq), F32)]
    return per_map * nmaps


def _flash_loop(qT_ref, k_ref, vT_ref, bias_fn, finalize_fn, scratch, *, nmaps, tq, tk):
    maps = [scratch[6 * mp:6 * mp + 6] for mp in range(nmaps)]
    S = k_ref.shape[0]
    n_chunks = S // tk
    n_items = (S // tq) * n_chunks
    assert n_chunks >= 2

    for _, acc_ref, _, _, _, _ in maps:
        acc_ref[...] = jnp.zeros(acc_ref.shape, F32)

    def scores(f):
        tile, c = f // n_chunks, f % n_chunks
        k_t = k_ref[pl.ds(pl.multiple_of(c * tk, tk), tk), :]
        bias = None if bias_fn is None else bias_fn(tile, c)
        for mp in range(nmaps):
            q_t = qT_ref[0, mp, :, pl.ds(pl.multiple_of(tile * tq, tq), tq)]
            s = jnp.dot(k_t, q_t, preferred_element_type=F32)
            if bias is not None:
                s = s + bias
            _, _, s_ref, _, _, smax_ref = maps[mp]
            s_ref[...] = s
            smax_ref[...] = jnp.max(s, axis=0, keepdims=True)

    def probs(f):
        first_chunk = (f % n_chunks) == 0
        for m_ref, _, s_ref, p_ref, alpha_ref, smax_ref in maps:
            m_prev = jnp.where(first_chunk, -jnp.inf, m_ref[...])
            m_new = jnp.maximum(m_prev, smax_ref[...])
            m_ref[...] = m_new
            alpha_ref[...] = jnp.exp2(m_prev - m_new)
            p_ref[...] = jnp.exp2(s_ref[...] - m_new).astype(BF16)

    def accumulate(f):
        c = f % n_chunks
        v_t = vT_ref[0, 0, :, pl.ds(pl.multiple_of(c * tk, tk), tk)]
        for _, acc_ref, _, p_ref, alpha_ref, _ in maps:
            acc_ref[...] = alpha_ref[...] * acc_ref[...] + jnp.dot(v_t, p_ref[...], preferred_element_type=F32)

    def finalize(tile):
        finalize_fn(tile, [acc_ref for _, acc_ref, _, _, _, _ in maps])

    scores(0)
    probs(0)
    scores(1)

    def body(f, carry):
        accumulate(f - 1)
        probs(f)
        scores(f + 1)

        @pl.when((f - 1) % n_chunks == n_chunks - 1)
        def _():
            finalize((f - 1) // n_chunks)

        return carry

    lax.fori_loop(1, n_items - 1, body, 0)
    accumulate(n_items - 2)
    probs(n_items - 1)
    accumulate(n_items - 1)
    finalize(S // tq - 1)


def _normalized(acc_ref):
    return acc_ref[:HEAD_W, :] * pl.reciprocal(acc_ref[HEAD_W:HEAD_W + 1, :])


def _diff_finalize(o1T, o2T, lambda_refs, subln_ref, o_ref, tile, tq, lambda_init):
    lq1_ref, lk1_ref, lq2_ref, lk2_ref = lambda_refs
    lam = (jnp.exp(jnp.sum(lq1_ref[...] * lk1_ref[...], axis=-1, keepdims=True))
           - jnp.exp(jnp.sum(lq2_ref[...] * lk2_ref[...], axis=-1, keepdims=True))
           + lambda_init)
    oT = o1T - lam * o2T
    ms = jnp.mean(oT * oT, axis=0, keepdims=True)
    oT = oT * lax.rsqrt(ms + EPS)
    rows = pl.ds(pl.multiple_of(tile * tq, tq), tq)
    o_ref[rows, :] = (oT.T * (subln_ref[...] * (1.0 - lambda_init))).astype(BF16)


def _attn_a_kernel(qT_ref, k_ref, vT_ref, bias_ref, lq1_ref, lk1_ref, lq2_ref, lk2_ref, subln_ref,
                   o_ref, *scratch, tq, tk, lambda_init):
    u = min(tq, tk)

    def bias_fn(tile, c):
        offset = (c * tk - tile * tq) // u
        idx = jnp.clip(offset, -(tk // u) - 1, tq // u + 1) + (tk // u + 1)
        return bias_ref[0, idx]

    def finalize_fn(tile, acc_refs):
        _diff_finalize(_normalized(acc_refs[0]), _normalized(acc_refs[1]),
                       (lq1_ref, lk1_ref, lq2_ref, lk2_ref), subln_ref, o_ref, tile, tq, lambda_init)

    _flash_loop(qT_ref, k_ref, vT_ref, bias_fn, finalize_fn, scratch, nmaps=2, tq=tq, tk=tk)


def _attn_a(qaT, ka, vaT, bias_tiles, lambdas, subln_w, lambda_init, tq, tk):
    B, H, _, S = vaT.shape
    nt = bias_tiles.shape[1]
    vec = pl.BlockSpec((1, A_QK_DIM), lambda b, h: (0, 0))
    kernel = functools.partial(_attn_a_kernel, tq=tq, tk=tk, lambda_init=lambda_init)
    return pl.pallas_call(
        kernel,
        out_shape=jax.ShapeDtypeStruct((B * S, H * HEAD_W), BF16),
        grid=(B, H),
        in_specs=[
            pl.BlockSpec((1, 2, HEAD_W, S), lambda b, h: (b, h, 0, 0)),
            pl.BlockSpec((S, HEAD_W), lambda b, h: (b, h)),
            pl.BlockSpec((1, 1, V_EXT_ROWS, S), lambda b, h: (b, h, 0, 0)),
            pl.BlockSpec((1, nt, tk, tq), lambda b, h: (h, 0, 0, 0), pipeline_mode=pl.Buffered(1)),
            vec, vec, vec, vec,
            pl.BlockSpec((1, A_V_DIM), lambda b, h: (0, 0)),
        ],
        out_specs=pl.BlockSpec((S, HEAD_W), lambda b, h: (b, h)),
        scratch_shapes=_flash_scratch(2, tq, tk, V_EXT_ROWS),
        compiler_params=_compiler_params(("arbitrary", "arbitrary")),
        name="attn_a",
    )(qaT, ka, vaT, bias_tiles, *lambdas, subln_w)


def _attn_b_kernel(qT_ref, k_ref, vT_ref, o_ref, *scratch, tq, tk):
    def finalize_fn(tile, acc_refs):
        rows = pl.ds(pl.multiple_of(tile * tq, tq), tq)
        o_ref[rows, :] = _normalized(acc_refs[0]).T.astype(BF16)

    _flash_loop(qT_ref, k_ref, vT_ref, None, finalize_fn, scratch, nmaps=1, tq=tq, tk=tk)


def _attn_b(qbT, kb, vbT, tq, tk):
    B, H, _, S = qbT.shape
    kernel = functools.partial(_attn_b_kernel, tq=tq, tk=tk)
    return pl.pallas_call(
        kernel,
        out_shape=jax.ShapeDtypeStruct((B * S, H * HEAD_W), BF16),
        grid=(B, H),
        in_specs=[
            pl.BlockSpec((1, 1, HEAD_W, S), lambda b, h: (b, h, 0, 0)),
            pl.BlockSpec((S, HEAD_W), lambda b, h: (b, h // B_GROUP)),
            pl.BlockSpec((1, 1, V_EXT_ROWS, S), lambda b, h: (b, h // B_GROUP, 0, 0)),
        ],
        out_specs=pl.BlockSpec((S, HEAD_W), lambda b, h: (b, h)),
        scratch_shapes=_flash_scratch(1, tq, tk, V_EXT_ROWS),
        compiler_params=_compiler_params(("arbitrary", "arbitrary")),
        name="attn_b",
    )(qbT, kb, vbT)


SCORE_BOUND = 60.0
NORM_MARGIN = 1.02


def _bounded_scratch(nmaps, tq, tk):
    return [pltpu.VMEM((HEAD_W, tq), F32), pltpu.VMEM((tk, tq), BF16), pltpu.VMEM((tk, tq), BF16),
            pltpu.VMEM((2, 1, tq), F32)] * nmaps


def _keep_unless_first(c):
    return jnp.where(c == 0, 0.0, 1.0).astype(F32)


def _bounded_loop(qT_ref, k_ref, vT_ref, scratch, shift_fn, finalize_fn, *, nmaps, tq, tk):
    maps = [scratch[4 * mp:4 * mp + 4] for mp in range(nmaps)]
    S = k_ref.shape[0]
    n_chunks, n_tiles = S // tk, S // tq
    n_items = n_tiles * n_chunks
    assert n_chunks % 2 == 0

    def accumulate(g, parity):
        c = g % n_chunks
        keep = _keep_unless_first(c)
        v_t = vT_ref[0, 0, :HEAD_W, pl.ds(pl.multiple_of(c * tk, tk), tk)]
        for refs in maps:
            acc_ref, p_ref = refs[0], refs[1 + parity]
            acc_ref[...] = keep * acc_ref[...] + jnp.dot(v_t, p_ref[...], preferred_element_type=F32)

    def probs(g, parity):
        tile, c = g // n_chunks, g % n_chunks
        keep = _keep_unless_first(c)
        shift = shift_fn(tile, c)
        k_t = k_ref[pl.ds(pl.multiple_of(c * tk, tk), tk), :]
        for mp, refs in enumerate(maps):
            p_ref, l_ref = refs[1 + parity], refs[3]
            q_t = qT_ref[0, mp, :, pl.ds(pl.multiple_of(tile * tq, tq), tq)]
            e = jnp.exp2(shift(jnp.dot(k_t, q_t, preferred_element_type=F32)))
            l_ref[tile % 2] = keep * l_ref[tile % 2] + jnp.sum(e, axis=0, keepdims=True)
            p_ref[...] = e.astype(BF16)

    def finalize(tile):
        finalize_fn(tile, [refs[0][...] * pl.reciprocal(refs[3][tile % 2]) for refs in maps])

    for refs in maps:
        refs[0][...] = jnp.zeros(refs[0].shape, F32)
        refs[3][...] = jnp.zeros(refs[3].shape, F32)

    def body(pair, carry):
        g = 2 * pair + 1
        probs(g, 1)
        accumulate(g - 1, 0)
        probs(g + 1, 0)
        accumulate(g, 1)

        @pl.when(g % n_chunks == n_chunks - 1)
        def _():
            finalize(g // n_chunks)

        return carry

    probs(0, 0)
    lax.fori_loop(0, (n_items - 2) // 2, body, 0)
    probs(n_items - 1, 1)
    accumulate(n_items - 2, 0)
    accumulate(n_items - 1, 1)
    finalize(n_tiles - 1)


def _attn_a_bounded_kernel(qT_ref, k_ref, vT_ref, bias_ref, lq1_ref, lk1_ref, lq2_ref, lk2_ref,
                           subln_ref, o_ref, *scratch, tq, tk, lambda_init):
    u = min(tq, tk)

    def shift_fn(tile, c):
        offset = (c * tk - tile * tq) // u
        idx = jnp.clip(offset, -(tk // u) - 1, tq // u + 1) + (tk // u + 1)
        return lambda s: s + bias_ref[0, idx].astype(F32)

    def finalize_fn(tile, o_maps):
        _diff_finalize(o_maps[0], o_maps[1], (lq1_ref, lk1_ref, lq2_ref, lk2_ref), subln_ref, o_ref, tile, tq,
                       lambda_init)

    _bounded_loop(qT_ref, k_ref, vT_ref, scratch, shift_fn, finalize_fn, nmaps=2, tq=tq, tk=tk)


def _attn_a_bounded(qaT, ka, vaT, bias_tiles, lambdas, subln_w, lambda_init, tq, tk):
    B, H, _, S = vaT.shape
    nt = bias_tiles.shape[1]
    vec = pl.BlockSpec((1, A_QK_DIM), lambda h, b: (0, 0))
    kernel = functools.partial(_attn_a_bounded_kernel, tq=tq, tk=tk, lambda_init=lambda_init)
    return pl.pallas_call(
        kernel,
        out_shape=jax.ShapeDtypeStruct((B * S, H * HEAD_W), BF16),
        grid=(H, B),
        in_specs=[
            pl.BlockSpec((1, 2, HEAD_W, S), lambda h, b: (b, h, 0, 0)),
            pl.BlockSpec((S, HEAD_W), lambda h, b: (b, h)),
            pl.BlockSpec((1, 1, V_EXT_ROWS, S), lambda h, b: (b, h, 0, 0)),
            pl.BlockSpec((1, nt, tk, tq), lambda h, b: (h, 0, 0, 0), pipeline_mode=pl.Buffered(1)),
            vec, vec, vec, vec,
            pl.BlockSpec((1, A_V_DIM), lambda h, b: (0, 0)),
        ],
        out_specs=pl.BlockSpec((S, HEAD_W), lambda h, b: (b, h)),
        scratch_shapes=_bounded_scratch(2, tq, tk),
        compiler_params=_compiler_params(("arbitrary", "arbitrary")),
        name="attn_a_bounded",
    )(qaT, ka, vaT, bias_tiles, *lambdas, subln_w)


def _attn_b_bounded_kernel(qT_ref, k_ref, vT_ref, o_ref, *scratch, tq, tk):
    def finalize_fn(tile, o_maps):
        rows = pl.ds(pl.multiple_of(tile * tq, tq), tq)
        o_ref[rows, :] = o_maps[0].T.astype(BF16)

    _bounded_loop(qT_ref, k_ref, vT_ref, scratch, lambda tile, c: (lambda s: s), finalize_fn,
                  nmaps=1, tq=tq, tk=tk)


def _attn_b_bounded(qbT, kb, vbT, tq, tk):
    B, H, _, S = qbT.shape
    kernel = functools.partial(_attn_b_bounded_kernel, tq=tq, tk=tk)
    return pl.pallas_call(
        kernel,
        out_shape=jax.ShapeDtypeStruct((B * S, H * HEAD_W), BF16),
        grid=(B, H),
        in_specs=[
            pl.BlockSpec((1, 1, HEAD_W, S), lambda b, h: (b, h, 0, 0)),
            pl.BlockSpec((S, HEAD_W), lambda b, h: (b, h // B_GROUP)),
            pl.BlockSpec((1, 1, V_EXT_ROWS, S), lambda b, h: (b, h // B_GROUP, 0, 0)),
        ],
        out_specs=pl.BlockSpec((S, HEAD_W), lambda b, h: (b, h)),
        scratch_shapes=_bounded_scratch(1, tq, tk),
        compiler_params=_compiler_params(("arbitrary", "arbitrary")),
        name="attn_b_bounded",
    )(qbT, kb, vbT)


def _out_proj_kernel(x_ref, oa_ref, za_ref, ob_ref, zb_ref, ga_ref, gb_ref, wpa_ref, wpb_ref, wo_ref,
                     gf_ref, y_ref):
    ya = jnp.dot(oa_ref[...] * za_ref[...], wpa_ref[...], preferred_element_type=F32)
    yb = jnp.dot(ob_ref[...] * zb_ref[...], wpb_ref[...], preferred_element_type=F32)
    merged = ga_ref[...].astype(F32) * ya + gb_ref[...].astype(F32) * yb
    h = x_ref[...] + jnp.dot(merged.astype(BF16), wo_ref[...], preferred_element_type=F32)
    ms = jnp.mean(h * h, axis=-1, keepdims=True)
    y_ref[...] = h * lax.rsqrt(ms + EPS) * gf_ref[...]


def _out_proj(x2, oa, za, ob, zb, ga, gb, w_proj_a, w_proj_b, w_out, g_final, tm):
    T, D = x2.shape
    tok = pl.BlockSpec((tm, D), lambda t: (t, 0))

    def const_spec(shape):
        return pl.BlockSpec(shape, lambda t: (0,) * len(shape), pipeline_mode=pl.Buffered(1))

    return pl.pallas_call(
        _out_proj_kernel,
        out_shape=jax.ShapeDtypeStruct((T, D), F32),
        grid=(T // tm,),
        in_specs=[tok] * 7 + [const_spec((D, D))] * 3 + [const_spec((1, D))],
        out_specs=tok,
        compiler_params=_compiler_params(("arbitrary",)),
        name="out_proj",
    )(x2, oa, za, ob, zb, ga, gb, w_proj_a, w_proj_b, w_out, g_final.reshape(1, D))


IN_PROJ_ROWS = 256
OUT_PROJ_ROWS = 512
ATTN_A_TQ = 512
ATTN_A_TK = 1024
ATTN_B_TQ = 512
ATTN_B_TK = 2048
ATTN_A_BOUNDED_TQ = 1024
ATTN_A_BOUNDED_TK = 1024
ATTN_B_BOUNDED_TQ = 1024
ATTN_B_BOUNDED_TK = 2048


def _layer(x, l, g_norm, w_in, lambdas, subln_w, q_norm_b, k_norm_b, w_proj_a, w_proj_b, w_out,
           bounded_bias_tiles, rel_bias, g_final):
    B, S, D = x.shape
    offs = [0]
    for n in IN_SIZES:
        offs.append(offs[-1] + n)
    w_bf = w_in.astype(BF16)
    w_sections = [w_bf[:, offs[k]:offs[k + 1]] for k in range(len(IN_SIZES))]
    qaT, ka, vaT, za, qbT, kb, vbT, zb, ga, gb, qa_sq, ka_sq = _in_proj(x, g_norm, w_sections, q_norm_b,
                                                                        k_norm_b, IN_PROJ_ROWS)
    lambda_init = 0.8 - 0.6 * math.exp(-0.3 * l)
    lambdas = [v.reshape(1, A_QK_DIM).astype(F32) for v in lambdas]
    subln = subln_w.reshape(1, A_V_DIM).astype(F32)

    bound_a = (jnp.sqrt(jnp.max(qa_sq) * jnp.max(ka_sq)) * NORM_MARGIN
               + jnp.max(jnp.abs(rel_bias)) * LOG2_E)
    oa = lax.cond(
        bound_a <= SCORE_BOUND,
        lambda: _attn_a_bounded(qaT, ka, vaT, bounded_bias_tiles, lambdas, subln, lambda_init,
                                ATTN_A_BOUNDED_TQ, ATTN_A_BOUNDED_TK),
        lambda: _attn_a(qaT, ka, vaT, _bias_tiles(rel_bias, ATTN_A_TQ, ATTN_A_TK, F32), lambdas, subln,
                        lambda_init, ATTN_A_TQ, ATTN_A_TK))
    bound_b = (B_HEAD_DIM * jnp.max(jnp.abs(q_norm_b)) * jnp.max(jnp.abs(k_norm_b))
               * (B_HEAD_DIM ** -0.5 * LOG2_E) * NORM_MARGIN)
    ob = lax.cond(
        bound_b <= SCORE_BOUND,
        lambda: _attn_b_bounded(qbT, kb, vbT, ATTN_B_BOUNDED_TQ, ATTN_B_BOUNDED_TK),
        lambda: _attn_b(qbT, kb, vbT, ATTN_B_TQ, ATTN_B_TK))
    y = _out_proj(x.reshape(B * S, D), oa, za, ob, zb, ga, gb, w_proj_a.astype(BF16), w_proj_b.astype(BF16),
                  w_out.astype(BF16), g_final, OUT_PROJ_ROWS)
    return y.reshape(B, S, D)


def kernel(x_prompt, x_sample, g_norm, w_in, lambda_q1, lambda_k1, lambda_q2, lambda_k2, subln_w,
           q_norm_b, k_norm_b, w_proj_a, w_proj_b, w_out, rel_bias, g_final):
    assert DEPTH == 1 and g_norm.shape[0] == DEPTH
    bounded_bias_tiles = _bias_tiles(rel_bias, ATTN_A_BOUNDED_TQ, ATTN_A_BOUNDED_TK, F32)
    l = 0
    outs = []
    for x in (x_prompt, x_sample):
        outs.append(_layer(x, l, g_norm[l], w_in[l],
                           (lambda_q1[l], lambda_k1[l], lambda_q2[l], lambda_k2[l]), subln_w[l],
                           q_norm_b[l], k_norm_b[l], w_proj_a[l], w_proj_b[l], w_out[l], bounded_bias_tiles,
                           rel_bias, g_final))
    return tuple(outs)
```

```python
import functools
import math

import jax
import jax.numpy as jnp
from jax import lax
from jax.experimental import pallas as pl
from jax.experimental.pallas import tpu as pltpu

D_MODEL = 1024
A_HEADS = 8
A_QK_DIM = 64
A_V_DIM = 128
B_HEADS = 8
B_KV_HEADS = 2
B_GROUP = B_HEADS // B_KV_HEADS
B_HEAD_DIM = 128
HEAD_W = 128
V_ONES_ROWS = 16
V_EXT_ROWS = HEAD_W + V_ONES_ROWS
ROPE_AXIS_DIM = 64
ROPE_THETA = 10000.0
GRID_W = 64
REL_BUCKETS = 32
REL_MAX_DIST = 128
EPS = 1e-6
DEPTH = 1
IN_SIZES = (1024, 1024, 1024, 1024, 1024, 256, 256, 1024, 1024, 1024)

BIAS_SATURATION_DIST = 128

VMEM_LIMIT_BYTES = 56 * 1024 * 1024

LOG2_E = math.log2(math.e)

F32 = jnp.float32
BF16 = jnp.bfloat16


def _compiler_params(semantics):
    return pltpu.CompilerParams(dimension_semantics=semantics, vmem_limit_bytes=VMEM_LIMIT_BYTES)


def _rel_bucket(rel):
    half = REL_BUCKETS // 2
    max_exact = half // 2
    ret = (rel > 0).astype(jnp.int32) * half
    n = jnp.abs(rel)
    nf = jnp.maximum(n, 1).astype(F32)
    large = max_exact + (jnp.log(nf / max_exact) / math.log(REL_MAX_DIST / max_exact)
                         * (half - max_exact)).astype(jnp.int32)
    large = jnp.minimum(large, half - 1)
    return ret + jnp.where(n < max_exact, n, large)


def _num_bias_tiles(tq, tk):
    u = min(tq, tk)
    return tk // u + tq // u + 3


def _bias_bucket_tiles(tq, tk):
    u = min(tq, tk)
    assert tq % u == 0 and tk % u == 0 and u >= BIAS_SATURATION_DIST
    nt = _num_bias_tiles(tq, tk)
    o = (jnp.arange(nt, dtype=jnp.int32) - (tk // u + 1)) * u
    kk = jnp.arange(tk, dtype=jnp.int32)
    qq = jnp.arange(tq, dtype=jnp.int32)
    rel = o[:, None, None] + kk[None, :, None] - qq[None, None, :]
    return _rel_bucket(rel)


def _tile_bucket_range(t, tq, tk):
    u = min(tq, tk)
    o = (t - (tk // u + 1)) * u
    half = REL_BUCKETS // 2
    if t == 0:
        return half - 1, half
    if t == _num_bias_tiles(tq, tk) - 1:
        return REL_BUCKETS - 1, REL_BUCKETS
    lo = 0 if o - tq + 1 <= 0 else half
    hi = REL_BUCKETS if o + tk - 1 > 0 else half
    return lo, hi


def _bias_tiles_kernel(bucket_ref, rel_bias_ref, out_ref, *, tq, tk):
    t, h = pl.program_id(0), pl.program_id(1)
    for ts in range(_num_bias_tiles(tq, tk)):
        lo, hi = _tile_bucket_range(ts, tq, tk)

        @pl.when(t == ts)
        def _():
            bucket = bucket_ref[0]
            acc = jnp.full(bucket.shape, rel_bias_ref[lo, h], F32)
            for b in range(lo + 1, hi):
                acc = jnp.where(bucket == b, rel_bias_ref[b, h], acc)
            out_ref[0, 0] = (acc * LOG2_E).astype(out_ref.dtype)


def _bias_tiles(rel_bias, tq, tk, dtype):
    buckets = _bias_bucket_tiles(tq, tk)
    nt = buckets.shape[0]
    return pl.pallas_call(
        functools.partial(_bias_tiles_kernel, tq=tq, tk=tk),
        out_shape=jax.ShapeDtypeStruct((A_HEADS, nt, tk, tq), dtype),
        grid=(nt, A_HEADS),
        in_specs=[
            pl.BlockSpec((1, tk, tq), lambda t, h: (t, 0, 0)),
            pl.BlockSpec(memory_space=pltpu.SMEM),
        ],
        out_specs=pl.BlockSpec((1, 1, tk, tq), lambda t, h: (h, t, 0, 0)),
        compiler_params=_compiler_params(("arbitrary", "arbitrary")),
        name="bias_tiles",
    )(buckets, rel_bias.astype(F32))


def _axial_tables(S):
    rows = S // GRID_W
    row_idx = jnp.repeat(jnp.arange(rows, dtype=F32), GRID_W)
    col_idx = jnp.tile(jnp.arange(GRID_W, dtype=F32), rows)
    inv_freq = ROPE_THETA ** (-jnp.arange(0, ROPE_AXIS_DIM, 2, dtype=F32) / ROPE_AXIS_DIM)
    ang_r = row_idx[:, None] * inv_freq[None, :]
    ang_c = col_idx[:, None] * inv_freq[None, :]
    cr, sr, cc, sc = jnp.cos(ang_r), jnp.sin(ang_r), jnp.cos(ang_c), jnp.sin(ang_c)
    cos_t = jnp.concatenate([cr, cr, cc, cc], axis=-1)
    sin_t = jnp.concatenate([-sr, sr, -sc, sc], axis=-1)
    return cos_t, sin_t


def _head_rms(t, gain):
    ms = jnp.mean(t * t, axis=-1, keepdims=True)
    return t * lax.rsqrt(ms + EPS) * gain


def _rope(t, cos_t, sin_t):
    lane = lax.broadcasted_iota(jnp.int32, t.shape, 1)
    quarter = ROPE_AXIS_DIM // 2
    partner = jnp.where((lane & quarter) == 0,
                        pltpu.roll(t, HEAD_W - quarter, 1), pltpu.roll(t, quarter, 1))
    return t * cos_t + partner * sin_t


def _in_proj_kernel(x_ref, g_ref, cos_ref, sin_ref, qn_ref, kn_ref,
                    w_qa, w_ka, w_va, w_za, w_qb, w_kb, w_vb, w_zb, w_ga, w_gb,
                    qaT_ref, ka_ref, vaT_ref, za_ref, qbT_ref, kb_ref, vbT_ref, zb_ref, ga_ref, gb_ref,
                    qa_sq_ref, ka_sq_ref):
    x = x_ref[...]
    ms = jnp.mean(x * x, axis=-1, keepdims=True)
    xn = (x * lax.rsqrt(ms + EPS) * g_ref[...]).astype(BF16)

    def proj(w_ref):
        return jnp.dot(xn, w_ref[...], preferred_element_type=F32)

    def head(p, h):
        return p[:, h * HEAD_W:(h + 1) * HEAD_W]

    def max_head_sq_norm(p):
        sq = p * p
        worst = jnp.sum(head(sq, 0), axis=-1, keepdims=True)
        for h in range(1, A_HEADS):
            worst = jnp.maximum(worst, jnp.sum(head(sq, h), axis=-1, keepdims=True))
        return jnp.broadcast_to(jnp.max(worst, axis=0, keepdims=True), (8, HEAD_W))

    def store_vT(ref, h, t):
        ref[0, h, :HEAD_W] = t.T.astype(BF16)
        ref[0, h, HEAD_W:] = jnp.ones((V_ONES_ROWS, t.shape[0]), BF16)

    p = proj(w_qa) * (A_QK_DIM ** -0.5 * LOG2_E)
    qa_sq_ref[0] = max_head_sq_norm(p)
    first_map = lax.broadcasted_iota(jnp.int32, (HEAD_W, x.shape[0]), 0) < A_QK_DIM
    for h in range(A_HEADS):
        t = head(p, h).T
        qaT_ref[0, 2 * h] = jnp.where(first_map, t, 0.0).astype(BF16)
        qaT_ref[0, 2 * h + 1] = jnp.where(first_map, 0.0, t).astype(BF16)
    p = proj(w_ka)
    ka_sq_ref[0] = max_head_sq_norm(p)
    ka_ref[...] = p.astype(BF16)
    p = proj(w_va)
    for h in range(A_HEADS):
        store_vT(vaT_ref, h, head(p, h))
    p = proj(w_za)
    za_ref[...] = (p * jax.nn.sigmoid(p)).astype(BF16)

    cos_t = cos_ref[...]
    sin_t = sin_ref[...]
    p = proj(w_qb)
    for h in range(B_HEADS):
        t = _rope(_head_rms(head(p, h), qn_ref[...]), cos_t, sin_t) * (B_HEAD_DIM ** -0.5 * LOG2_E)
        qbT_ref[0, h] = t.T.astype(BF16)
    p = proj(w_kb)
    for h in range(B_KV_HEADS):
        t = _rope(_head_rms(head(p, h), kn_ref[...]), cos_t, sin_t)
        kb_ref[:, h * HEAD_W:(h + 1) * HEAD_W] = t.astype(BF16)
    p = proj(w_vb)
    for h in range(B_KV_HEADS):
        store_vT(vbT_ref, h, head(p, h))
    p = proj(w_zb)
    zb_ref[...] = (p * jax.nn.sigmoid(p)).astype(BF16)
    ga_ref[...] = jax.nn.sigmoid(proj(w_ga)).astype(BF16)
    gb_ref[...] = jax.nn.sigmoid(proj(w_gb)).astype(BF16)


def _in_proj(x, g_norm, w_sections, q_norm_b, k_norm_b, tm):
    B, S, D = x.shape
    T = B * S
    assert S % tm == 0
    tiles_per_seq = S // tm
    cos_t, sin_t = _axial_tables(S)
    x2 = x.reshape(T, D)

    def tok_spec(width):
        return pl.BlockSpec((tm, width), lambda t: (t, 0))

    def headT_spec(nh, rows=HEAD_W):
        return pl.BlockSpec((1, nh, rows, tm), lambda t: (t // tiles_per_seq, 0, 0, t % tiles_per_seq))

    def const_spec(shape):
        return pl.BlockSpec(shape, lambda t: (0,) * len(shape), pipeline_mode=pl.Buffered(1))

    rope_spec = pl.BlockSpec((tm, HEAD_W), lambda t: (t % tiles_per_seq, 0))
    in_specs = [tok_spec(D), const_spec((1, D)), rope_spec, rope_spec,
                const_spec((1, HEAD_W)), const_spec((1, HEAD_W))]
    in_specs += [const_spec(w.shape) for w in w_sections]
    sq_out = jax.ShapeDtypeStruct((T // tm, 8, HEAD_W), F32)
    sq_spec = pl.BlockSpec((1, 8, HEAD_W), lambda t: (t, 0, 0))

    def tok_out(width):
        return jax.ShapeDtypeStruct((T, width), BF16)

    def headT_out(nh, rows=HEAD_W):
        return jax.ShapeDtypeStruct((B, nh, rows, S), BF16)

    out_shape = [headT_out(2 * A_HEADS), tok_out(1024), headT_out(A_HEADS, V_EXT_ROWS), tok_out(1024),
                 headT_out(B_HEADS), tok_out(256), headT_out(B_KV_HEADS, V_EXT_ROWS), tok_out(1024),
                 tok_out(1024), tok_out(1024), sq_out, sq_out]
    out_specs = [headT_spec(2 * A_HEADS), tok_spec(1024), headT_spec(A_HEADS, V_EXT_ROWS), tok_spec(1024),
                 headT_spec(B_HEADS), tok_spec(256), headT_spec(B_KV_HEADS, V_EXT_ROWS), tok_spec(1024),
                 tok_spec(1024), tok_spec(1024), sq_spec, sq_spec]
    return pl.pallas_call(
        _in_proj_kernel,
        out_shape=out_shape,
        grid=(T // tm,),
        in_specs=in_specs,
        out_specs=out_specs,
        compiler_params=_compiler_params(("arbitrary",)),
        name="in_proj",
    )(x2, g_norm.reshape(1, D), cos_t, sin_t, q_norm_b.reshape(1, HEAD_W), k_norm_b.reshape(1, HEAD_W),
      *w_sections)


def _flash_scratch(nmaps, tq, tk, dv):
    per_map = [pltpu.VMEM((1, tq), F32), pltpu.VMEM((dv, tq), F32),
               pltpu.VMEM((tk, tq), F32), pltpu.VMEM((tk, tq), BF16), pltpu.VMEM((1, tq), F32),
               pltpu.VMEM((1, tq), F32)]
    return per_map * nmaps


def _flash_loop(qT_ref, k_ref, vT_ref, bias_fn, finalize_fn, scratch, *, nmaps, tq, tk):
    maps = [scratch[6 * mp:6 * mp + 6] for mp in range(nmaps)]
    S = k_ref.shape[0]
    n_chunks = S // tk
    n_items = (S // tq) * n_chunks
    assert n_chunks >= 2

    for _, acc_ref, _, _, _, _ in maps:
        acc_ref[...] = jnp.zeros(acc_ref.shape, F32)

    def scores(f):
        tile, c = f // n_chunks, f % n_chunks
        k_t = k_ref[pl.ds(pl.multiple_of(c * tk, tk), tk), :]
        bias = None if bias_fn is None else bias_fn(tile, c)
        for mp in range(nmaps):
            q_t = qT_ref[0, mp, :, pl.ds(pl.multiple_of(tile * tq, tq), tq)]
            s = jnp.dot(k_t, q_t, preferred_element_type=F32)
            if bias is not None:
                s = s + bias
            _, _, s_ref, _, _, smax_ref = maps[mp]
            s_ref[...] = s
            smax_ref[...] = jnp.max(s, axis=0, keepdims=True)

    def probs(f):
        first_chunk = (f % n_chunks) == 0
        for m_ref, _, s_ref, p_ref, alpha_ref, smax_ref in maps:
            m_prev = jnp.where(first_chunk, -jnp.inf, m_ref[...])
            m_new = jnp.maximum(m_prev, smax_ref[...])
            m_ref[...] = m_new
            alpha_ref[...] = jnp.exp2(m_prev - m_new)
            p_ref[...] = jnp.exp2(s_ref[...] - m_new).astype(BF16)

    def accumulate(f):
        c = f % n_chunks
        v_t = vT_ref[0, 0, :, pl.ds(pl.multiple_of(c * tk, tk), tk)]
        for _, acc_ref, _, p_ref, alpha_ref, _ in maps:
            acc_ref[...] = alpha_ref[...] * acc_ref[...] + jnp.dot(v_t, p_ref[...], preferred_element_type=F32)

    def finalize(tile):
        finalize_fn(tile, [acc_ref for _, acc_ref, _, _, _, _ in maps])

    scores(0)
    probs(0)
    scores(1)

    def body(f, carry):
        accumulate(f - 1)
        probs(f)
        scores(f + 1)

        @pl.when((f - 1) % n_chunks == n_chunks - 1)
        def _():
            finalize((f - 1) // n_chunks)

        return carry

    lax.fori_loop(1, n_items - 1, body, 0)
    accumulate(n_items - 2)
    probs(n_items - 1)
    accumulate(n_items - 1)
    finalize(S // tq - 1)


def _normalized(acc_ref):
    return acc_ref[:HEAD_W, :] * pl.reciprocal(acc_ref[HEAD_W:HEAD_W + 1, :])


def _diff_finalize(o1T, o2T, lambda_refs, subln_ref, o_ref, tile, tq, lambda_init):
    lq1_ref, lk1_ref, lq2_ref, lk2_ref = lambda_refs
    lam = (jnp.exp(jnp.sum(lq1_ref[...] * lk1_ref[...], axis=-1, keepdims=True))
           - jnp.exp(jnp.sum(lq2_ref[...] * lk2_ref[...], axis=-1, keepdims=True))
           + lambda_init)
    oT = o1T - lam * o2T
    ms = jnp.mean(oT * oT, axis=0, keepdims=True)
    oT = oT * lax.rsqrt(ms + EPS)
    rows = pl.ds(pl.multiple_of(tile * tq, tq), tq)
    o_ref[rows, :] = (oT.T * (subln_ref[...] * (1.0 - lambda_init))).astype(BF16)


def _attn_a_kernel(qT_ref, k_ref, vT_ref, bias_ref, lq1_ref, lk1_ref, lq2_ref, lk2_ref, subln_ref,
                   o_ref, *scratch, tq, tk, lambda_init):
    u = min(tq, tk)

    def bias_fn(tile, c):
        offset = (c * tk - tile * tq) // u
        idx = jnp.clip(offset, -(tk // u) - 1, tq // u + 1) + (tk // u + 1)
        return bias_ref[0, idx]

    def finalize_fn(tile, acc_refs):
        _diff_finalize(_normalized(acc_refs[0]), _normalized(acc_refs[1]),
                       (lq1_ref, lk1_ref, lq2_ref, lk2_ref), subln_ref, o_ref, tile, tq, lambda_init)

    _flash_loop(qT_ref, k_ref, vT_ref, bias_fn, finalize_fn, scratch, nmaps=2, tq=tq, tk=tk)


def _attn_a(qaT, ka, vaT, bias_tiles, lambdas, subln_w, lambda_init, tq, tk):
    B, H, _, S = vaT.shape
    nt = bias_tiles.shape[1]
    vec = pl.BlockSpec((1, A_QK_DIM), lambda b, h: (0, 0))
    kernel = functools.partial(_attn_a_kernel, tq=tq, tk=tk, lambda_init=lambda_init)
    return pl.pallas_call(
        kernel,
        out_shape=jax.ShapeDtypeStruct((B * S, H * HEAD_W), BF16),
        grid=(B, H),
        in_specs=[
            pl.BlockSpec((1, 2, HEAD_W, S), lambda b, h: (b, h, 0, 0)),
            pl.BlockSpec((S, HEAD_W), lambda b, h: (b, h)),
            pl.BlockSpec((1, 1, V_EXT_ROWS, S), lambda b, h: (b, h, 0, 0)),
            pl.BlockSpec((1, nt, tk, tq), lambda b, h: (h, 0, 0, 0), pipeline_mode=pl.Buffered(1)),
            vec, vec, vec, vec,
            pl.BlockSpec((1, A_V_DIM), lambda b, h: (0, 0)),
        ],
        out_specs=pl.BlockSpec((S, HEAD_W), lambda b, h: (b, h)),
        scratch_shapes=_flash_scratch(2, tq, tk, V_EXT_ROWS),
        compiler_params=_compiler_params(("arbitrary", "arbitrary")),
        name="attn_a",
    )(qaT, ka, vaT, bias_tiles, *lambdas, subln_w)


def _attn_b_kernel(qT_ref, k_ref, vT_ref, o_ref, *scratch, tq, tk):
    def finalize_fn(tile, acc_refs):
        rows = pl.ds(pl.multiple_of(tile * tq, tq), tq)
        o_ref[rows, :] = _normalized(acc_refs[0]).T.astype(BF16)

    _flash_loop(qT_ref, k_ref, vT_ref, None, finalize_fn, scratch, nmaps=1, tq=tq, tk=tk)


def _attn_b(qbT, kb, vbT, tq, tk):
    B, H, _, S = qbT.shape
    kernel = functools.partial(_attn_b_kernel, tq=tq, tk=tk)
    return pl.pallas_call(
        kernel,
        out_shape=jax.ShapeDtypeStruct((B * S, H * HEAD_W), BF16),
        grid=(B, H),
        in_specs=[
            pl.BlockSpec((1, 1, HEAD_W, S), lambda b, h: (b, h, 0, 0)),
            pl.BlockSpec((S, HEAD_W), lambda b, h: (b, h // B_GROUP)),
            pl.BlockSpec((1, 1, V_EXT_ROWS, S), lambda b, h: (b, h // B_GROUP, 0, 0)),
        ],
        out_specs=pl.BlockSpec((S, HEAD_W), lambda b, h: (b, h)),
        scratch_shapes=_flash_scratch(1, tq, tk, V_EXT_ROWS),
        compiler_params=_compiler_params(("arbitrary", "arbitrary")),
        name="attn_b",
    )(qbT, kb, vbT)


SCORE_BOUND = 60.0
NORM_MARGIN = 1.02


def _bounded_scratch(nmaps, tq, tk):
    return [pltpu.VMEM((HEAD_W, tq), F32), pltpu.VMEM((tk, tq), BF16), pltpu.VMEM((tk, tq), BF16),
            pltpu.VMEM((2, 1, tq), F32)] * nmaps


def _keep_unless_first(c):
    return jnp.where(c == 0, 0.0, 1.0).astype(F32)


def _bounded_loop(load_q, load_k, load_v, scratch, shift_fn, finalize_fn, *, nmaps, n_groups, n_tiles, n_chunks):
    maps = [scratch[4 * mp:4 * mp + 4] for mp in range(nmaps)]
    n_rows = n_groups * n_tiles
    n_items = n_rows * n_chunks
    assert n_chunks % 2 == 0

    def accumulate(g, parity):
        row, c = g // n_chunks, g % n_chunks
        keep = _keep_unless_first(c)
        v_t = load_v(row // n_tiles, c)
        for refs in maps:
            acc_ref, p_ref = refs[0], refs[1 + parity]
            acc_ref[...] = keep * acc_ref[...] + jnp.dot(v_t, p_ref[...], preferred_element_type=F32)

    def probs(g, parity):
        row, c = g // n_chunks, g % n_chunks
        group, tile = row // n_tiles, row % n_tiles
        keep = _keep_unless_first(c)
        shift = shift_fn(tile, c)
        k_t = load_k(group, c)
        for mp, refs in enumerate(maps):
            p_ref, l_ref = refs[1 + parity], refs[3]
            e = jnp.exp2(shift(jnp.dot(k_t, load_q(group, mp, tile), preferred_element_type=F32)))
            l_ref[row % 2] = keep * l_ref[row % 2] + jnp.sum(e, axis=0, keepdims=True)
            p_ref[...] = e.astype(BF16)

    def finalize(row):
        finalize_fn(row // n_tiles, row % n_tiles,
                    [refs[0][...] * pl.reciprocal(refs[3][row % 2]) for refs in maps])

    for refs in maps:
        refs[0][...] = jnp.zeros(refs[0].shape, F32)
        refs[3][...] = jnp.zeros(refs[3].shape, F32)

    def body(pair, carry):
        g = 2 * pair + 1
        probs(g, 1)
        accumulate(g - 1, 0)
        probs(g + 1, 0)
        accumulate(g, 1)

        @pl.when(g % n_chunks == n_chunks - 1)
        def _():
            finalize(g // n_chunks)

        return carry

    probs(0, 0)
    lax.fori_loop(0, (n_items - 2) // 2, body, 0)
    probs(n_items - 1, 1)
    accumulate(n_items - 2, 0)
    accumulate(n_items - 1, 1)
    finalize(n_rows - 1)


def _aligned_block(start, size):
    return pl.ds(pl.multiple_of(start, size), size)


def _attn_a_bounded_kernel(qT_ref, k_ref, vT_ref, bias_ref, lq1_ref, lk1_ref, lq2_ref, lk2_ref,
                           subln_ref, o_ref, *scratch, tq, tk, lambda_init):
    u = min(tq, tk)

    def shift_fn(tile, c):
        offset = (c * tk - tile * tq) // u
        idx = jnp.clip(offset, -(tk // u) - 1, tq // u + 1) + (tk // u + 1)
        return lambda s: s + bias_ref[0, idx].astype(F32)

    def finalize_fn(group, tile, o_maps):
        _diff_finalize(o_maps[0], o_maps[1], (lq1_ref, lk1_ref, lq2_ref, lk2_ref), subln_ref, o_ref, tile, tq,
                       lambda_init)

    S = k_ref.shape[0]
    _bounded_loop(lambda group, mp, tile: qT_ref[0, mp, :, _aligned_block(tile * tq, tq)],
                  lambda group, c: k_ref[_aligned_block(c * tk, tk), :],
                  lambda group, c: vT_ref[0, 0, :HEAD_W, _aligned_block(c * tk, tk)],
                  scratch, shift_fn, finalize_fn, nmaps=2, n_groups=1, n_tiles=S // tq, n_chunks=S // tk)


def _attn_a_bounded(qaT, ka, vaT, bias_tiles, lambdas, subln_w, lambda_init, tq, tk):
    B, H, _, S = vaT.shape
    nt = bias_tiles.shape[1]
    vec = pl.BlockSpec((1, A_QK_DIM), lambda h, b: (0, 0))
    kernel = functools.partial(_attn_a_bounded_kernel, tq=tq, tk=tk, lambda_init=lambda_init)
    return pl.pallas_call(
        kernel,
        out_shape=jax.ShapeDtypeStruct((B * S, H * HEAD_W), BF16),
        grid=(H, B),
        in_specs=[
            pl.BlockSpec((1, 2, HEAD_W, S), lambda h, b: (b, h, 0, 0)),
            pl.BlockSpec((S, HEAD_W), lambda h, b: (b, h)),
            pl.BlockSpec((1, 1, V_EXT_ROWS, S), lambda h, b: (b, h, 0, 0)),
            pl.BlockSpec((1, nt, tk, tq), lambda h, b: (h, 0, 0, 0), pipeline_mode=pl.Buffered(1)),
            vec, vec, vec, vec,
            pl.BlockSpec((1, A_V_DIM), lambda h, b: (0, 0)),
        ],
        out_specs=pl.BlockSpec((S, HEAD_W), lambda h, b: (b, h)),
        scratch_shapes=_bounded_scratch(2, tq, tk),
        compiler_params=_compiler_params(("arbitrary", "arbitrary")),
        name="attn_a_bounded",
    )(qaT, ka, vaT, bias_tiles, *lambdas, subln_w)


def _attn_b_bounded_kernel(qT_ref, k_ref, vT_ref, o_ref, *scratch, tq, tk):
    def finalize_fn(head, tile, o_maps):
        o_ref[_aligned_block(tile * tq, tq), _aligned_block(head * HEAD_W, HEAD_W)] = o_maps[0].T.astype(BF16)

    S = k_ref.shape[0]
    _bounded_loop(lambda head, mp, tile: qT_ref[0, head, :, _aligned_block(tile * tq, tq)],
                  lambda head, c: k_ref[_aligned_block(c * tk, tk), :],
                  lambda head, c: vT_ref[0, 0, :HEAD_W, _aligned_block(c * tk, tk)],
                  scratch, lambda tile, c: (lambda s: s), finalize_fn,
                  nmaps=1, n_groups=qT_ref.shape[1], n_tiles=S // tq, n_chunks=S // tk)


def _attn_b_bounded(qbT, kb, vbT, tq, tk, heads_per_step):
    B, H, _, S = qbT.shape
    assert B_GROUP % heads_per_step == 0
    steps_per_kv_head = B_GROUP // heads_per_step
    kernel = functools.partial(_attn_b_bounded_kernel, tq=tq, tk=tk)
    return pl.pallas_call(
        kernel,
        out_shape=jax.ShapeDtypeStruct((B * S, H * HEAD_W), BF16),
        grid=(B, H // heads_per_step),
        in_specs=[
            pl.BlockSpec((1, heads_per_step, HEAD_W, S), lambda b, j: (b, j, 0, 0)),
            pl.BlockSpec((S, HEAD_W), lambda b, j: (b, j // steps_per_kv_head)),
            pl.BlockSpec((1, 1, V_EXT_ROWS, S), lambda b, j: (b, j // steps_per_kv_head, 0, 0)),
        ],
        out_specs=pl.BlockSpec((S, heads_per_step * HEAD_W), lambda b, j: (b, j)),
        scratch_shapes=_bounded_scratch(1, tq, tk),
        compiler_params=_compiler_params(("arbitrary", "arbitrary")),
        name="attn_b_bounded",
    )(qbT, kb, vbT)


def _out_proj_kernel(x_ref, oa_ref, za_ref, ob_ref, zb_ref, ga_ref, gb_ref, wpa_ref, wpb_ref, wo_ref,
                     gf_ref, y_ref):
    ya = jnp.dot(oa_ref[...] * za_ref[...], wpa_ref[...], preferred_element_type=F32)
    yb = jnp.dot(ob_ref[...] * zb_ref[...], wpb_ref[...], preferred_element_type=F32)
    merged = ga_ref[...].astype(F32) * ya + gb_ref[...].astype(F32) * yb
    h = x_ref[...] + jnp.dot(merged.astype(BF16), wo_ref[...], preferred_element_type=F32)
    ms = jnp.mean(h * h, axis=-1, keepdims=True)
    y_ref[...] = h * lax.rsqrt(ms + EPS) * gf_ref[...]


def _out_proj(x2, oa, za, ob, zb, ga, gb, w_proj_a, w_proj_b, w_out, g_final, tm):
    T, D = x2.shape
    tok = pl.BlockSpec((tm, D), lambda t: (t, 0))

    def const_spec(shape):
        return pl.BlockSpec(shape, lambda t: (0,) * len(shape), pipeline_mode=pl.Buffered(1))

    return pl.pallas_call(
        _out_proj_kernel,
        out_shape=jax.ShapeDtypeStruct((T, D), F32),
        grid=(T // tm,),
        in_specs=[tok] * 7 + [const_spec((D, D))] * 3 + [const_spec((1, D))],
        out_specs=tok,
        compiler_params=_compiler_params(("arbitrary",)),
        name="out_proj",
    )(x2, oa, za, ob, zb, ga, gb, w_proj_a, w_proj_b, w_out, g_final.reshape(1, D))


IN_PROJ_ROWS = 256
OUT_PROJ_ROWS = 512
ATTN_A_TQ = 512
ATTN_A_TK = 1024
ATTN_B_TQ = 512
ATTN_B_TK = 2048
ATTN_A_BOUNDED_TQ = 1024
ATTN_A_BOUNDED_TK = 1024
ATTN_B_BOUNDED_TQ = 1024
ATTN_B_BOUNDED_TK = 2048
ATTN_B_WHOLE_GROUP_MAX_SEQ = 4096


def _layer(x, l, g_norm, w_in, lambdas, subln_w, q_norm_b, k_norm_b, w_proj_a, w_proj_b, w_out,
           bounded_bias_tiles, rel_bias, g_final):
    B, S, D = x.shape
    offs = [0]
    for n in IN_SIZES:
        offs.append(offs[-1] + n)
    w_bf = w_in.astype(BF16)
    w_sections = [w_bf[:, offs[k]:offs[k + 1]] for k in range(len(IN_SIZES))]
    qaT, ka, vaT, za, qbT, kb, vbT, zb, ga, gb, qa_sq, ka_sq = _in_proj(x, g_norm, w_sections, q_norm_b,
                                                                        k_norm_b, IN_PROJ_ROWS)
    lambda_init = 0.8 - 0.6 * math.exp(-0.3 * l)
    lambdas = [v.reshape(1, A_QK_DIM).astype(F32) for v in lambdas]
    subln = subln_w.reshape(1, A_V_DIM).astype(F32)

    bound_a = (jnp.sqrt(jnp.max(qa_sq) * jnp.max(ka_sq)) * NORM_MARGIN
               + jnp.max(jnp.abs(rel_bias)) * LOG2_E)
    oa = lax.cond(
        bound_a <= SCORE_BOUND,
        lambda: _attn_a_bounded(qaT, ka, vaT, bounded_bias_tiles, lambdas, subln, lambda_init,
                                ATTN_A_BOUNDED_TQ, ATTN_A_BOUNDED_TK),
        lambda: _attn_a(qaT, ka, vaT, _bias_tiles(rel_bias, ATTN_A_TQ, ATTN_A_TK, F32), lambdas, subln,
                        lambda_init, ATTN_A_TQ, ATTN_A_TK))
    bound_b = (B_HEAD_DIM * jnp.max(jnp.abs(q_norm_b)) * jnp.max(jnp.abs(k_norm_b))
               * (B_HEAD_DIM ** -0.5 * LOG2_E) * NORM_MARGIN)
    ob = lax.cond(
        bound_b <= SCORE_BOUND,
        lambda: _attn_b_bounded(qbT, kb, vbT, ATTN_B_BOUNDED_TQ, ATTN_B_BOUNDED_TK,
                                B_GROUP if S <= ATTN_B_WHOLE_GROUP_MAX_SEQ else B_GROUP // 2),
        lambda: _attn_b(qbT, kb, vbT, ATTN_B_TQ, ATTN_B_TK))
    y = _out_proj(x.reshape(B * S, D), oa, za, ob, zb, ga, gb, w_proj_a.astype(BF16), w_proj_b.astype(BF16),
                  w_out.astype(BF16), g_final, OUT_PROJ_ROWS)
    return y.reshape(B, S, D)


def kernel(x_prompt, x_sample, g_norm, w_in, lambda_q1, lambda_k1, lambda_q2, lambda_k2, subln_w,
           q_norm_b, k_norm_b, w_proj_a, w_proj_b, w_out, rel_bias, g_final):
    assert DEPTH == 1 and g_norm.shape[0] == DEPTH
    bounded_bias_tiles = _bias_tiles(rel_bias, ATTN_A_BOUNDED_TQ, ATTN_A_BOUNDED_TK, F32)
    l = 0
    outs = []
    for x in (x_prompt, x_sample):
        outs.append(_layer(x, l, g_norm[l], w_in[l],
                           (lambda_q1[l], lambda_k1[l], lambda_q2[l], lambda_k2[l]), subln_w[l],
                           q_norm_b[l], k_norm_b[l], w_proj_a[l], w_proj_b[l], w_out[l], bounded_bias_tiles,
                           rel_bias, g_final))
    return tuple(outs)
```
